```python
import jax, jax.numpy as jnp
from jax import lax
import numpy as np

D_MODEL = 1024
BATCH = 32
SEQ = 256
DEPTH = 4
DEC_BATCH = 2
DEC_SEQ = 1024
PAST_LEN = 256

GRID_W = 64
N_MIXERS = 2
EXPAND = 2
D_INNER = EXPAND * D_MODEL
CONV_K = 31
RET_HEADS = 4
RET_DK = D_MODEL // RET_HEADS
RET_DV = D_INNER // RET_HEADS
CHUNK = 128
ROPE_BASE = 10000.0
EPS = 1e-6
N_CONV = (DEPTH + 1) // 2
N_RET = DEPTH // 2

kernel_name = "conformer_retention_prefix_dit"


def rms_norm(x, g):
    xf = x.astype(jnp.float32)
    y = xf * lax.rsqrt(jnp.mean(xf * xf, axis=-1, keepdims=True) + EPS)
    return (y * g.astype(jnp.float32)).astype(x.dtype)


def layer_norm(x, g, b):
    xf = x.astype(jnp.float32)
    mu = jnp.mean(xf, axis=-1, keepdims=True)
    var = jnp.mean(jnp.square(xf - mu), axis=-1, keepdims=True)
    y = (xf - mu) * lax.rsqrt(var + EPS) * g.astype(jnp.float32) + b.astype(jnp.float32)
    return y.astype(x.dtype)


def head_norm(o, g):
    of = o.astype(jnp.float32)
    mu = jnp.mean(of, axis=-1, keepdims=True)
    var = jnp.mean(jnp.square(of - mu), axis=-1, keepdims=True)
    y = (of - mu) * lax.rsqrt(var + EPS) * g.astype(jnp.float32).reshape(RET_HEADS, RET_DV)
    return y.astype(o.dtype)


def adaln(cond, w, b):
    m = jax.nn.silu(cond) @ w + b
    return jnp.split(m, 3, axis=-1)


def rope_2d(L, dtype):
    rows = L // GRID_W
    r = jnp.repeat(jnp.arange(rows, dtype=jnp.float32), GRID_W)
    col = jnp.tile(jnp.arange(GRID_W, dtype=jnp.float32), rows)
    nf = RET_DK // 4
    inv = ROPE_BASE ** (-jnp.arange(nf, dtype=jnp.float32) / nf)
    ang = jnp.concatenate([r[:, None] * inv, col[:, None] * inv], axis=-1)
    return jnp.cos(ang).astype(dtype), jnp.sin(ang).astype(dtype)


def apply_rope(x, cos, sin):
    half = RET_DK // 2
    x1, x2 = x[..., :half], x[..., half:]
    cs, sn = cos[None, :, None, :], sin[None, :, None, :]
    return jnp.concatenate([x1 * cs - x2 * sn, x2 * cs + x1 * sn], axis=-1)


def conv_mixer(h, w_in, dw, dw_b, ln_g, ln_b, w_out):
    a, b, g = jnp.split(h @ w_in, 3, axis=-1)
    u = a * jax.nn.sigmoid(b)
    u = lax.conv_general_dilated(
        u, dw[:, None, :].astype(u.dtype), window_strides=(1,),
        padding=[(CONV_K // 2, CONV_K // 2)],
        dimension_numbers=('NWC', 'WIO', 'NWC'),
        feature_group_count=D_INNER) + dw_b
    u = jax.nn.silu(layer_norm(u, ln_g, ln_b))
    return (u * jax.nn.silu(g)) @ w_out


def retention_scan(q, k, v, log_gamma, s0):
    B, L = q.shape[0], q.shape[1]
    n = L // CHUNK
    dt = q.dtype
    qc = q.reshape(B, n, CHUNK, RET_HEADS, RET_DK)
    kc = k.reshape(B, n, CHUNK, RET_HEADS, RET_DK)
    vc = v.reshape(B, n, CHUNK, RET_HEADS, RET_DV)
    pos = jnp.arange(CHUNK, dtype=jnp.float32)
    diff = pos[:, None] - pos[None, :]
    decay_in = jnp.where(diff[..., None] >= 0,
                         jnp.exp(jnp.maximum(diff, 0.0)[..., None] * log_gamma), 0.0)
    scores = jnp.einsum('bnihk,bnjhk->bnhij', qc, kc) * jnp.transpose(decay_in, (2, 0, 1)).astype(dt)
    o_in = jnp.einsum('bnhij,bnjhv->bnihv', scores, vc)
    w_k = jnp.exp((CHUNK - 1 - pos)[:, None] * log_gamma).astype(dt)
    kv = jnp.einsum('bnjhk,jh,bnjhv->bnhkv', kc, w_k, vc)
    g_chunk = jnp.exp(CHUNK * log_gamma).astype(dt)[None, :, None, None]

    def step(s, kv_n):
        return g_chunk * s + kv_n, s

    s_fin, s_prev = lax.scan(step, s0.astype(dt), jnp.moveaxis(kv, 1, 0))
    s_prev = jnp.moveaxis(s_prev, 0, 1)
    w_q = jnp.exp((pos + 1.0)[:, None] * log_gamma).astype(dt)
    o_x = jnp.einsum('bnihk,ih,bnhkv->bnihv', qc, w_q, s_prev)
    return (o_in + o_x).reshape(B, L, RET_HEADS, RET_DV), s_fin


def retention_mixer(h, w_in, log_decay, gn_g, w_out, s_init, rope):
    B, L = h.shape[0], h.shape[1]
    q, k, v, g = jnp.split(h @ w_in, [D_MODEL, 2 * D_MODEL, 2 * D_MODEL + D_INNER], axis=-1)
    q = q.reshape(B, L, RET_HEADS, RET_DK)
    k = k.reshape(B, L, RET_HEADS, RET_DK) * (RET_DK ** -0.5)
    v = v.reshape(B, L, RET_HEADS, RET_DV)
    if rope is not None:
        q = apply_rope(q, *rope)
        k = apply_rope(k, *rope)
    if s_init is None:
        s_init = jnp.zeros((B, 2, RET_HEADS, RET_DK, RET_DV), q.dtype)
    log_gamma = -jnp.exp(log_decay.astype(jnp.float32))
    o_f, s_f = retention_scan(q, k, v, log_gamma[0], s_init[:, 0])
    o_b, s_b = retention_scan(q[:, ::-1], k[:, ::-1], v[:, ::-1], log_gamma[1], s_init[:, 1])
    o = head_norm(o_f + o_b[:, ::-1], gn_g).reshape(B, L, D_INNER)
    return (o * jax.nn.silu(g)) @ w_out, jnp.stack([s_f, s_b], axis=1)


def setup_inputs(seed: int = 0) -> dict:
    key = jax.random.key(seed)
    ks = jax.random.split(key, 24)
    f32 = jnp.float32
    nrm = lambda k, shape, s: jax.random.normal(k, shape, f32) * s
    base_decay = np.log(-np.log(1.0 - 2.0 ** (-5.0 - np.arange(RET_HEADS)))).astype(np.float32)
    return {
        "x_prompt": nrm(ks[0], (BATCH, SEQ, D_MODEL), 1.0),
        "x_sample": nrm(ks[1], (DEC_BATCH, DEC_SEQ, D_MODEL), 1.0),
        "c": nrm(ks[2], (DEC_BATCH, D_MODEL), 1.0),
        "state_ret": nrm(ks[3], (DEC_BATCH, N_RET, 2, RET_HEADS, RET_DK, RET_DV), 0.5),
        "c_ctx": nrm(ks[4], (D_MODEL,), 1.0),
        "ada_w": nrm(ks[5], (DEPTH, D_MODEL, 3 * D_MODEL), D_MODEL ** -0.5),
        "ada_b": nrm(ks[6], (DEPTH, 3 * D_MODEL), 0.01),
        "norm_g": 1.0 + nrm(ks[7], (DEPTH, D_MODEL), 0.01),
        "final_norm_g": 1.0 + nrm(ks[8], (D_MODEL,), 0.01),
        "conv_w_in": nrm(ks[9], (N_CONV, D_MODEL, 3 * D_INNER), D_MODEL ** -0.5),
        "conv_dw": nrm(ks[10], (N_CONV, CONV_K, D_INNER), CONV_K ** -0.5),
        "conv_dw_b": nrm(ks[11], (N_CONV, D_INNER), 0.01),
        "conv_ln_g": 1.0 + nrm(ks[12], (N_CONV, D_INNER), 0.01),
        "conv_ln_b": nrm(ks[13], (N_CONV, D_INNER), 0.01),
        "conv_w_out": nrm(ks[14], (N_CONV, D_INNER, D_MODEL), D_INNER ** -0.5),
        "ret_w_in": nrm(ks[15], (N_RET, D_MODEL, 2 * D_MODEL + 2 * D_INNER), D_MODEL ** -0.5),
        "ret_log_decay": jnp.asarray(base_decay)[None, None, :] + nrm(ks[16], (N_RET, 2, RET_HEADS), 0.1),
        "ret_gn_g": 1.0 + nrm(ks[17], (N_RET, D_INNER), 0.01),
        "ret_w_out": nrm(ks[18], (N_RET, D_INNER, D_MODEL), D_INNER ** -0.5),
    }


def reference(x_prompt, x_sample, c, state_ret, c_ctx, ada_w, ada_b, norm_g, final_norm_g,
              conv_w_in, conv_dw, conv_dw_b, conv_ln_g, conv_ln_b, conv_w_out,
              ret_w_in, ret_log_decay, ret_gn_g, ret_w_out):
    rope = rope_2d(x_sample.shape[1], x_sample.dtype)
    xp, xs = x_prompt, x_sample
    new_states = []
    for i in range(DEPTH):
        j = i // N_MIXERS
        sh_p, sc_p, gt_p = adaln(c_ctx[None, None, :], ada_w[i], ada_b[i])
        sh_s, sc_s, gt_s = adaln(c[:, None, :], ada_w[i], ada_b[i])
        hp = rms_norm(xp, norm_g[i]) * (1.0 + sc_p) + sh_p
        hs = rms_norm(xs, norm_g[i]) * (1.0 + sc_s) + sh_s
        if i % N_MIXERS == 0:
            op = conv_mixer(hp, conv_w_in[j], conv_dw[j], conv_dw_b[j], conv_ln_g[j], conv_ln_b[j], conv_w_out[j])
            os_ = conv_mixer(hs, conv_w_in[j], conv_dw[j], conv_dw_b[j], conv_ln_g[j], conv_ln_b[j], conv_w_out[j])
        else:
            op, s_ctx = retention_mixer(hp, ret_w_in[j], ret_log_decay[j], ret_gn_g[j], ret_w_out[j], None, None)
            os_, _ = retention_mixer(hs, ret_w_in[j], ret_log_decay[j], ret_gn_g[j], ret_w_out[j],
                                     state_ret[:, j], rope)
            new_states.append(s_ctx)
        xp = xp + gt_p * op
        xs = xs + gt_s * os_
    y_prompt = rms_norm(xp, final_norm_g)
    y_sample = rms_norm(xs, final_norm_g)
    new_state_ret = jnp.stack(new_states, axis=1)
    return (y_prompt, y_sample, new_state_ret)
```

```python
import functools

import jax
import jax.numpy as jnp
from jax import lax
from jax.experimental import pallas as pl
from jax.experimental.pallas import tpu as pltpu

D_MODEL = 1024
D_INNER = 2048
DEPTH = 4
CONV_K = 31
HALO = 16
RET_HEADS = 4
RET_DK = 256
RET_DV = 512
GRID_W = 64
ROPE_BASE = 10000.0
EPS = 1e-6
TILE = 256
LANES = 128
CHUNK = 512
VMEM_LIMIT = 56 * 1024 * 1024

F32 = jnp.float32
BF16 = jnp.bfloat16


def _dot(a, b):
    return jnp.dot(a, b, preferred_element_type=F32)


def _dot_nt(a, b):
    return lax.dot_general(a, b, (((1,), (1,)), ((), ())), preferred_element_type=F32)


def _sigmoid(x):
    return 1.0 / (1.0 + jnp.exp(-x))


def _silu(x):
    return x * _sigmoid(x)


def _rms(x, g):
    return x * lax.rsqrt(jnp.mean(x * x, axis=-1, keepdims=True) + EPS) * g


def _mod_norm(x, g, mod_ref):
    return _rms(x, g) * (1.0 + mod_ref[0, 1:2, :]) + mod_ref[0, 0:1, :]


def _adaln_kernel(cond_ref, w_ref, b_ref, o_ref):
    s = _silu(cond_ref[...]).astype(BF16)
    o_ref[0] = _dot(s, w_ref[0].astype(BF16)) + b_ref[0]


def _adaln(cond8, ada_w, ada_b):
    nb = 3 * D_MODEL // D_MODEL
    return pl.pallas_call(
        _adaln_kernel,
        grid=(DEPTH, nb),
        in_specs=[
            pl.BlockSpec((8, D_MODEL), lambda i, n: (0, 0)),
            pl.BlockSpec((1, D_MODEL, D_MODEL), lambda i, n: (i, 0, n)),
            pl.BlockSpec((1, 1, D_MODEL), lambda i, n: (i, 0, n)),
        ],
        out_specs=pl.BlockSpec((1, 8, D_MODEL), lambda i, n: (i, 0, n)),
        out_shape=jax.ShapeDtypeStruct((DEPTH, 8, 3 * D_MODEL), F32),
        compiler_params=pltpu.CompilerParams(
            dimension_semantics=("arbitrary", "arbitrary"), vmem_limit_bytes=VMEM_LIMIT),
        name="adaln",
    )(cond8, ada_w, ada_b.reshape(DEPTH, 1, 3 * D_MODEL))


def _conv_kernel(*refs, halo, tiles_per_seq):
    if halo:
        (x_ref, xt_ref, xb_ref, mod_ref, ng_ref, win_ref, dw_ref, dwb_ref, lng_ref, lnb_ref,
         wout_ref, o_ref, pad_ref, u_ref, gate_ref) = refs
    else:
        (x_ref, mod_ref, ng_ref, win_ref, dw_ref, dwb_ref, lng_ref, lnb_ref,
         wout_ref, o_ref, pad_ref, u_ref, gate_ref) = refs
    n_slab = CHUNK // LANES
    ng = ng_ref[...]
    h = _mod_norm(x_ref[0], ng, mod_ref).astype(BF16)
    if halo:
        t = pl.program_id(1)
        ht = _mod_norm(xt_ref[0], ng, mod_ref).astype(BF16)
        hb = _mod_norm(xb_ref[0], ng, mod_ref).astype(BF16)
        hext = jnp.concatenate([ht, h, hb], axis=0)
        row = lax.broadcasted_iota(jnp.int32, (TILE + 2 * HALO, CHUNK), 0)
        valid = jnp.logical_and(jnp.logical_or(row >= HALO, t > 0),
                                jnp.logical_or(row < HALO + TILE, t < tiles_per_seq - 1))
    else:
        hext = h
        zeros = jnp.zeros((HALO, LANES), F32)
        for s in range(D_INNER // LANES):
            pad_ref[s, 0:HALO, :] = zeros
            pad_ref[s, HALO + TILE:2 * HALO + TILE, :] = zeros

    for c in range(D_INNER // CHUNK):
        c0 = c * CHUNK
        a = _dot(hext, win_ref[:, c0:c0 + CHUNK])
        b = _dot(hext, win_ref[:, D_INNER + c0:D_INNER + c0 + CHUNK])
        g = _dot(h, win_ref[:, 2 * D_INNER + c0:2 * D_INNER + c0 + CHUNK])
        u = a * _sigmoid(b)
        gate_ref[:, c0:c0 + CHUNK] = _silu(g)
        if halo:
            u = jnp.where(valid, u, 0.0)
        for l in range(n_slab):
            s = c * n_slab + l
            if halo:
                pad_ref[s, :, :] = u[:, l * LANES:(l + 1) * LANES]
            else:
                pad_ref[s, HALO:HALO + TILE, :] = u[:, l * LANES:(l + 1) * LANES]
        for l in range(n_slab):
            s = c * n_slab + l
            lanes = slice(c0 + l * LANES, c0 + (l + 1) * LANES)
            for rb in range(TILE // LANES):
                acc = jnp.broadcast_to(dwb_ref[0:1, lanes], (LANES, LANES))
                for k in range(CONV_K):
                    start = rb * LANES + k + HALO - CONV_K // 2
                    acc = acc + pad_ref[s, pl.ds(start, LANES), :] * dw_ref[k:k + 1, lanes]
                u_ref[rb * LANES:(rb + 1) * LANES, lanes] = acc

    u = u_ref[...]
    mu = jnp.mean(u, axis=-1, keepdims=True)
    d = u - mu
    var = jnp.mean(d * d, axis=-1, keepdims=True)
    y = d * lax.rsqrt(var + EPS) * lng_ref[...] + lnb_ref[...]
    z = (_silu(y) * gate_ref[...]).astype(BF16)
    out = _dot(z, wout_ref[...])
    o_ref[0] = x_ref[0] + mod_ref[0, 2:3, :] * out


def _const_spec(shape):
    nd = len(shape)
    return pl.BlockSpec(shape, lambda *_: (0,) * nd, pipeline_mode=pl.Buffered(1))


def _conv_layer(x, mod, ng, w_in, dw, dwb, lng, lnb, w_out, *, per_batch_mod):
    B, L, _ = x.shape
    tiles = L // TILE
    halo = tiles > 1
    hb = TILE // HALO
    x_spec = pl.BlockSpec((1, TILE, D_MODEL), lambda b, t: (b, t, 0))
    in_specs = [x_spec]
    args = [x]
    if halo:
        in_specs += [
            pl.BlockSpec((1, HALO, D_MODEL), lambda b, t: (b, jnp.maximum(t * hb - 1, 0), 0)),
            pl.BlockSpec((1, HALO, D_MODEL), lambda b, t: (b, jnp.minimum((t + 1) * hb, L // HALO - 1), 0)),
        ]
        args += [x, x]
    mod_map = (lambda b, t: (b, 0, 0)) if per_batch_mod else (lambda b, t: (0, 0, 0))
    in_specs += [
        pl.BlockSpec((1, 3, D_MODEL), mod_map),
        _const_spec((1, D_MODEL)),
        _const_spec((D_MODEL, 3 * D_INNER)),
        _const_spec((CONV_K, D_INNER)),
        _const_spec((1, D_INNER)),
        _const_spec((1, D_INNER)),
        _const_spec((1, D_INNER)),
        _const_spec((D_INNER, D_MODEL)),
    ]
    args += [mod, ng, w_in, dw, dwb, lng, lnb, w_out]
    return pl.pallas_call(
        functools.partial(_conv_kernel, halo=halo, tiles_per_seq=tiles),
        grid=(B, tiles),
        in_specs=in_specs,
        out_specs=x_spec,
        out_shape=jax.ShapeDtypeStruct(x.shape, F32),
        scratch_shapes=[
            pltpu.VMEM((D_INNER // LANES, TILE + 2 * HALO, LANES), F32),
            pltpu.VMEM((TILE, D_INNER), F32),
            pltpu.VMEM((TILE, D_INNER), F32),
        ],
        compiler_params=pltpu.CompilerParams(
            dimension_semantics=("arbitrary", "arbitrary"), vmem_limit_bytes=VMEM_LIMIT),
        name="conv_sample" if halo else "conv_prompt",
    )(*args)


def _decay_mask(diff, lgf, lgb):
    e = jnp.exp(jnp.abs(diff) * jnp.where(diff >= 0.0, lgf, lgb))
    return e * jnp.where(diff == 0.0, 2.0 * RET_DK ** -0.5, RET_DK ** -0.5)


def _head_norm_gate(o, gn, g):
    mu = jnp.mean(o, axis=-1, keepdims=True)
    d = o - mu
    var = jnp.mean(d * d, axis=-1, keepdims=True)
    return (d * lax.rsqrt(var + EPS) * gn * _silu(g)).astype(BF16)


def _ret_prompt_kernel(*refs, last):
    if last:
        (x_ref, mod_ref, ng_ref, win_ref, ld_ref, gn_ref, wout_ref, fg_ref, _st_in,
         o_ref, st_ref, m_ref, w_ref, z_ref) = refs
    else:
        (x_ref, mod_ref, ng_ref, win_ref, ld_ref, gn_ref, wout_ref,
         o_ref, st_ref, m_ref, w_ref, z_ref) = refs
    L = TILE

    @pl.when(pl.program_id(0) == 0)
    def _():
        lg = -jnp.exp(ld_ref[...])
        ii = lax.broadcasted_iota(jnp.int32, (L, L), 0)
        jj = lax.broadcasted_iota(jnp.int32, (L, L), 1)
        diff = (ii - jj).astype(F32)
        pos = lax.broadcasted_iota(jnp.int32, (1, L), 1).astype(F32)
        for hd in range(RET_HEADS):
            lgf = lg[0:1, hd:hd + 1]
            lgb = lg[1:2, hd:hd + 1]
            m_ref[hd] = _decay_mask(diff, lgf, lgb)
            w_ref[2 * hd:2 * hd + 1, :] = jnp.exp((L - 1.0 - pos) * lgf) * RET_DK ** -0.5
            w_ref[2 * hd + 1:2 * hd + 2, :] = jnp.exp(pos * lgb) * RET_DK ** -0.5

    x = x_ref[0]
    h = _mod_norm(x, ng_ref[...], mod_ref).astype(BF16)
    q = _dot(h, win_ref[:, 0:D_MODEL]).astype(BF16)
    k = _dot(h, win_ref[:, D_MODEL:2 * D_MODEL])
    v = _dot(h, win_ref[:, 2 * D_MODEL:2 * D_MODEL + D_INNER]).astype(BF16)
    g = _dot(h, win_ref[:, 2 * D_MODEL + D_INNER:])
    kb = k.astype(BF16)
    for hd in range(RET_HEADS):
        kcols = slice(hd * RET_DK, (hd + 1) * RET_DK)
        vcols = slice(hd * RET_DV, (hd + 1) * RET_DV)
        vh = v[:, vcols]
        s = _dot_nt(q[:, kcols], kb[:, kcols])
        p = (s * m_ref[hd]).astype(BF16)
        o = _dot(p, vh)
        z_ref[:, vcols] = _head_norm_gate(o, gn_ref[0:1, vcols], g[:, vcols])
        kt = k[:, kcols].T
        st_ref[0, 0, 0, hd] = _dot((kt * w_ref[2 * hd:2 * hd + 1, :]).astype(BF16), vh)
        st_ref[0, 0, 1, hd] = _dot((kt * w_ref[2 * hd + 1:2 * hd + 2, :]).astype(BF16), vh)
    xn = x + mod_ref[0, 2:3, :] * _dot(z_ref[...], wout_ref[...])
    if last:
        xn = _rms(xn, fg_ref[...])
    o_ref[0] = xn


def _ret_prompt_layer(x, mod, ng, w_in, ld, gn, w_out, final_g, states, *, j):
    B, L, _ = x.shape
    last = final_g is not None
    x_spec = pl.BlockSpec((1, L, D_MODEL), lambda b: (b, 0, 0))
    st_shape = (B, DEPTH // 2, 2, RET_HEADS, RET_DK, RET_DV)
    in_specs = [
        x_spec,
        pl.BlockSpec((1, 3, D_MODEL), lambda b: (0, 0, 0)),
        _const_spec((1, D_MODEL)),
        _const_spec((D_MODEL, 2 * D_MODEL + 2 * D_INNER)),
        _const_spec((2, RET_HEADS)),
        _const_spec((1, D_INNER)),
        _const_spec((D_INNER, D_MODEL)),
    ]
    args = [x, mod, ng, w_in, ld, gn, w_out]
    aliases = {}
    if last:
        in_specs += [_const_spec((1, D_MODEL)), pl.BlockSpec(memory_space=pl.ANY)]
        args += [final_g, states]
        aliases = {len(args) - 1: 1}
    return pl.pallas_call(
        functools.partial(_ret_prompt_kernel, last=last),
        grid=(B,),
        in_specs=in_specs,
        out_specs=[x_spec,
                   pl.BlockSpec((1, 1, 2, RET_HEADS, RET_DK, RET_DV), lambda b: (b, j, 0, 0, 0, 0))],
        out_shape=[jax.ShapeDtypeStruct(x.shape, F32), jax.ShapeDtypeStruct(st_shape, F32)],
        scratch_shapes=[
            pltpu.VMEM((RET_HEADS, L, L), F32),
            pltpu.VMEM((2 * RET_HEADS, L), F32),
            pltpu.VMEM((L, D_INNER), BF16),
        ],
        input_output_aliases=aliases,
        compiler_params=pltpu.CompilerParams(
            dimension_semantics=("arbitrary",), vmem_limit_bytes=VMEM_LIMIT),
        name="ret_prompt",
    )(*args)


def _ret_sample_kernel(*refs, last):
    if last:
        (x_ref, mod_ref, ng_ref, wq_ref, wk_ref, wv_ref, wg_ref, ld_ref, gn_ref, wout_ref, s0_ref,
         cos_ref, sin_ref, fg_ref, o_ref, h_ref) = refs
    else:
        (x_ref, mod_ref, ng_ref, wq_ref, wk_ref, wv_ref, wg_ref, ld_ref, gn_ref, wout_ref, s0_ref,
         cos_ref, sin_ref, o_ref, h_ref) = refs
    L = x_ref.shape[1]
    half = RET_DK // 2
    hd = pl.program_id(1)

    @pl.when(hd == 0)
    def _():
        x = x_ref[0]
        h_ref[...] = _mod_norm(x, ng_ref[...], mod_ref).astype(BF16)
        o_ref[0] = x

    h = h_ref[...]
    lgf = -jnp.exp(jnp.full((1, L), ld_ref[0, hd], F32))
    lgb = -jnp.exp(jnp.full((1, L), ld_ref[1, hd], F32))
    cos = cos_ref[...]
    sin = sin_ref[...]

    def rope(t):
        t1, t2 = t[:, :half], t[:, half:]
        return jnp.concatenate([t1 * cos - t2 * sin, t2 * cos + t1 * sin], axis=-1)

    qh = rope(_dot(h, wq_ref[...]))
    kh = rope(_dot(h, wk_ref[...])).astype(BF16)
    vh = _dot(h, wv_ref[...]).astype(BF16)
    gh = _dot(h, wg_ref[...])
    posq = lax.broadcasted_iota(jnp.int32, (L, RET_DK), 0).astype(F32)
    lgf_q = -jnp.exp(jnp.full((1, RET_DK), ld_ref[0, hd], F32))
    lgb_q = -jnp.exp(jnp.full((1, RET_DK), ld_ref[1, hd], F32))
    ox = (_dot((qh * jnp.exp((posq + 1.0) * lgf_q)).astype(BF16), s0_ref[0, 0, 0, 0].astype(BF16))
          + _dot((qh * jnp.exp((L - posq) * lgb_q)).astype(BF16), s0_ref[0, 0, 1, 0].astype(BF16)))
    qb = qh.astype(BF16)
    gate = mod_ref[0, 2:3, :]
    for r in range(L // TILE):
        rows = slice(r * TILE, (r + 1) * TILE)
        ii = lax.broadcasted_iota(jnp.int32, (TILE, L), 0) + r * TILE
        jj = lax.broadcasted_iota(jnp.int32, (TILE, L), 1)
        m = _decay_mask((ii - jj).astype(F32), lgf, lgb)
        p = (_dot_nt(qb[rows], kh) * m).astype(BF16)
        o = _dot(p, vh) + ox[rows]
        z = _head_norm_gate(o, gn_ref[...], gh[rows])
        o_ref[0, rows, :] += gate * _dot(z, wout_ref[...])

    if last:
        @pl.when(hd == RET_HEADS - 1)
        def _():
            o_ref[0] = _rms(o_ref[0], fg_ref[...])


def _ret_sample_layer(x, mod, ng, w_in, ld, gn, w_out, state_ret, cos, sin, final_g, *, j):
    B, L, _ = x.shape
    last = final_g is not None
    x_spec = pl.BlockSpec((1, L, D_MODEL), lambda b, hd: (b, 0, 0))
    kb = D_MODEL // RET_DK
    vb = 2 * D_MODEL // RET_DV
    in_specs = [
        x_spec,
        pl.BlockSpec((1, 3, D_MODEL), lambda b, hd: (b, 0, 0)),
        _const_spec((1, D_MODEL)),
        pl.BlockSpec((D_MODEL, RET_DK), lambda b, hd: (0, hd)),
        pl.BlockSpec((D_MODEL, RET_DK), lambda b, hd: (0, kb + hd)),
        pl.BlockSpec((D_MODEL, RET_DV), lambda b, hd: (0, vb + hd)),
        pl.BlockSpec((D_MODEL, RET_DV), lambda b, hd: (0, vb + RET_HEADS + hd)),
        pl.BlockSpec(memory_space=pltpu.SMEM),
        pl.BlockSpec((1, RET_DV), lambda b, hd: (0, hd)),
        pl.BlockSpec((RET_DV, D_MODEL), lambda b, hd: (hd, 0)),
        pl.BlockSpec((1, 1, 2, 1, RET_DK, RET_DV), lambda b, hd: (b, j, 0, hd, 0, 0)),
        _const_spec((L, RET_DK // 2)),
        _const_spec((L, RET_DK // 2)),
    ]
    args = [x, mod, ng, w_in, w_in, w_in, w_in, ld, gn, w_out, state_ret, cos, sin]
    if last:
        in_specs.append(_const_spec((1, D_MODEL)))
        args.append(final_g)
    return pl.pallas_call(
        functools.partial(_ret_sample_kernel, last=last),
        grid=(B, RET_HEADS),
        in_specs=in_specs,
        out_specs=x_spec,
        out_shape=jax.ShapeDtypeStruct(x.shape, F32),
        scratch_shapes=[pltpu.VMEM((L, D_MODEL), BF16)],
        compiler_params=pltpu.CompilerParams(
            dimension_semantics=("arbitrary", "arbitrary"), vmem_limit_bytes=VMEM_LIMIT),
        name="ret_sample",
    )(*args)


def _rope_tables(L):
    rows = L // GRID_W
    r = jnp.repeat(jnp.arange(rows, dtype=F32), GRID_W)
    col = jnp.tile(jnp.arange(GRID_W, dtype=F32), rows)
    nf = RET_DK // 4
    inv = ROPE_BASE ** (-jnp.arange(nf, dtype=F32) / nf)
    ang = jnp.concatenate([r[:, None] * inv, col[:, None] * inv], axis=-1)
    return jnp.cos(ang), jnp.sin(ang)


def kernel(x_prompt, x_sample, c, state_ret, c_ctx, ada_w, ada_b, norm_g, final_norm_g, conv_w_in, conv_dw, conv_dw_b, conv_ln_g, conv_ln_b, conv_w_out, ret_w_in, ret_log_decay, ret_gn_g, ret_w_out):
    n_dec = c.shape[0]
    cond8 = jnp.concatenate([c_ctx[None, :], c, jnp.zeros((8 - 1 - n_dec, D_MODEL), F32)], axis=0)
    mods = _adaln(cond8, ada_w, ada_b).reshape(DEPTH, 8, 3, D_MODEL)
    cos, sin = _rope_tables(x_sample.shape[1])
    fg = final_norm_g.reshape(1, D_MODEL)

    xp, xs = x_prompt, x_sample
    states = None
    for i in range(DEPTH):
        j = i // 2
        mod_p = mods[i, 0:1]
        mod_s = mods[i, 1:1 + n_dec]
        ng = norm_g[i].reshape(1, D_MODEL)
        if i % 2 == 0:
            w_in = conv_w_in[j].astype(BF16)
            w_out = conv_w_out[j].astype(BF16)
            rest = (ng, w_in, conv_dw[j], conv_dw_b[j].reshape(1, D_INNER), conv_ln_g[j].reshape(1, D_INNER),
                    conv_ln_b[j].reshape(1, D_INNER), w_out)
            xp = _conv_layer(xp, mod_p, *rest, per_batch_mod=False)
            xs = _conv_layer(xs, mod_s, *rest, per_batch_mod=True)
        else:
            w_in = ret_w_in[j].astype(BF16)
            w_out = ret_w_out[j].astype(BF16)
            gn = ret_gn_g[j].reshape(1, D_INNER)
            final_g = fg if i == DEPTH - 1 else None
            xp, states = _ret_prompt_layer(xp, mod_p, ng, w_in, ret_log_decay[j], gn, w_out, final_g, states, j=j)
            xs = _ret_sample_layer(xs, mod_s, ng, w_in, ret_log_decay[j], gn, w_out, state_ret, cos, sin,
                                   final_g, j=j)
    return (xp, xs, states)
```

```python
import functools

import jax
import jax.numpy as jnp
from jax import lax
from jax.experimental import pallas as pl
from jax.experimental.pallas import tpu as pltpu

D_MODEL = 1024
D_INNER = 2048
DEPTH = 4
CONV_K = 31
HALO = 16
RET_HEADS = 4
RET_DK = 256
RET_DV = 512
GRID_W = 64
ROPE_BASE = 10000.0
EPS = 1e-6
TILE = 256
LANES = 128
CHUNK = 512
VMEM_LIMIT = 56 * 1024 * 1024

F32 = jnp.float32
BF16 = jnp.bfloat16


def _dot(a, b):
    return jnp.dot(a, b, preferred_element_type=F32)


def _dot_nt(a, b):
    return lax.dot_general(a, b, (((1,), (1,)), ((), ())), preferred_element_type=F32)


def _sigmoid(x):
    return 1.0 / (1.0 + jnp.exp(-x))


def _silu(x):
    return x * _sigmoid(x)


def _rms(x, g):
    return x * lax.rsqrt(jnp.mean(x * x, axis=-1, keepdims=True) + EPS) * g


def _mod_norm(x, g, mod_ref):
    return _rms(x, g) * (1.0 + mod_ref[0, 1:2, :]) + mod_ref[0, 0:1, :]


def _adaln_kernel(cond_ref, w_ref, b_ref, o_ref):
    s = _silu(cond_ref[...]).astype(BF16)
    o_ref[0] = _dot(s, w_ref[0].astype(BF16)) + b_ref[0]


def _adaln(cond8, ada_w, ada_b):
    nb = 3 * D_MODEL // D_MODEL
    return pl.pallas_call(
        _adaln_kernel,
        grid=(DEPTH, nb),
        in_specs=[
            pl.BlockSpec((8, D_MODEL), lambda i, n: (0, 0)),
            pl.BlockSpec((1, D_MODEL, D_MODEL), lambda i, n: (i, 0, n)),
            pl.BlockSpec((1, 1, D_MODEL), lambda i, n: (i, 0, n)),
        ],
        out_specs=pl.BlockSpec((1, 8, D_MODEL), lambda i, n: (i, 0, n)),
        out_shape=jax.ShapeDtypeStruct((DEPTH, 8, 3 * D_MODEL), F32),
        compiler_params=pltpu.CompilerParams(
            dimension_semantics=("arbitrary", "arbitrary"), vmem_limit_bytes=VMEM_LIMIT),
        name="adaln",
    )(cond8, ada_w, ada_b.reshape(DEPTH, 1, 3 * D_MODEL))


def _conv_kernel(*refs, halo, tiles_per_seq):
    if halo:
        (x_ref, xt_ref, xb_ref, mod_ref, ng_ref, win_ref, dw_ref, dwb_ref, lng_ref, lnb_ref,
         wout_ref, o_ref, pad_ref, u_ref, gate_ref) = refs
    else:
        (x_ref, mod_ref, ng_ref, win_ref, dw_ref, dwb_ref, lng_ref, lnb_ref,
         wout_ref, o_ref, pad_ref, u_ref, gate_ref) = refs
    n_slab = CHUNK // LANES
    ng = ng_ref[...]
    h = _mod_norm(x_ref[0], ng, mod_ref).astype(BF16)
    if halo:
        t = pl.program_id(1)
        ht = _mod_norm(xt_ref[0], ng, mod_ref).astype(BF16)
        hb = _mod_norm(xb_ref[0], ng, mod_ref).astype(BF16)
        hext = jnp.concatenate([ht, h, hb], axis=0)
        row = lax.broadcasted_iota(jnp.int32, (TILE + 2 * HALO, CHUNK), 0)
        valid = jnp.logical_and(jnp.logical_or(row >= HALO, t > 0),
                                jnp.logical_or(row < HALO + TILE, t < tiles_per_seq - 1))
    else:
        hext = h
        zeros = jnp.zeros((HALO, LANES), F32)
        for s in range(D_INNER // LANES):
            pad_ref[s, 0:HALO, :] = zeros
            pad_ref[s, HALO + TILE:2 * HALO + TILE, :] = zeros

    for c in range(D_INNER // CHUNK):
        c0 = c * CHUNK
        a = _dot(hext, win_ref[:, c0:c0 + CHUNK])
        b = _dot(hext, win_ref[:, D_INNER + c0:D_INNER + c0 + CHUNK])
        g = _dot(h, win_ref[:, 2 * D_INNER + c0:2 * D_INNER + c0 + CHUNK])
        u = a * _sigmoid(b)
        gate_ref[:, c0:c0 + CHUNK] = _silu(g)
        if halo:
            u = jnp.where(valid, u, 0.0)
        for l in range(n_slab):
            s = c * n_slab + l
            if halo:
                pad_ref[s, :, :] = u[:, l * LANES:(l + 1) * LANES]
            else:
                pad_ref[s, HALO:HALO + TILE, :] = u[:, l * LANES:(l + 1) * LANES]
        for l in range(n_slab):
            s = c * n_slab + l
            lanes = slice(c0 + l * LANES, c0 + (l + 1) * LANES)
            for rb in range(TILE // LANES):
                acc = jnp.broadcast_to(dwb_ref[0:1, lanes], (LANES, LANES))
                for k in range(CONV_K):
                    start = rb * LANES + k + HALO - CONV_K // 2
                    acc = acc + pad_ref[s, pl.ds(start, LANES), :] * dw_ref[k:k + 1, lanes]
                u_ref[rb * LANES:(rb + 1) * LANES, lanes] = acc

    u = u_ref[...]
    mu = jnp.mean(u, axis=-1, keepdims=True)
    d = u - mu
    var = jnp.mean(d * d, axis=-1, keepdims=True)
    y = d * lax.rsqrt(var + EPS) * lng_ref[...] + lnb_ref[...]
    z = (_silu(y) * gate_ref[...]).astype(BF16)
    out = _dot(z, wout_ref[...])
    o_ref[0] = x_ref[0] + mod_ref[0, 2:3, :] * out


def _const_spec(shape, layer=None):
    nd = len(shape)
    if layer is None:
        return pl.BlockSpec(shape, lambda *_: (0,) * nd, pipeline_mode=pl.Buffered(1))
    return pl.BlockSpec((None,) + shape, lambda *_: (layer,) + (0,) * nd, pipeline_mode=pl.Buffered(1))


def _conv_layer(x, mods, ng, w_in, dw, dwb, lng, lnb, w_out, *, i, mod_row):
    B, L, _ = x.shape
    j = i // 2
    tiles = L // TILE
    halo = tiles > 1
    hb = TILE // HALO
    x_spec = pl.BlockSpec((1, TILE, D_MODEL), lambda b, t: (b, t, 0))
    in_specs = [x_spec]
    args = [x]
    if halo:
        in_specs += [
            pl.BlockSpec((1, HALO, D_MODEL), lambda b, t: (b, jnp.maximum(t * hb - 1, 0), 0)),
            pl.BlockSpec((1, HALO, D_MODEL), lambda b, t: (b, jnp.minimum((t + 1) * hb, L // HALO - 1), 0)),
        ]
        args += [x, x]
    mod_map = (lambda b, t: (i, mod_row + b, 0, 0)) if mod_row else (lambda b, t: (i, 0, 0, 0))
    in_specs += [
        pl.BlockSpec((None, 1, 3, D_MODEL), mod_map),
        _const_spec((1, D_MODEL), i),
        _const_spec((D_MODEL, 3 * D_INNER), j),
        _const_spec((CONV_K, D_INNER), j),
        _const_spec((1, D_INNER), j),
        _const_spec((1, D_INNER), j),
        _const_spec((1, D_INNER), j),
        _const_spec((D_INNER, D_MODEL), j),
    ]
    args += [mods, ng, w_in, dw, dwb, lng, lnb, w_out]
    return pl.pallas_call(
        functools.partial(_conv_kernel, halo=halo, tiles_per_seq=tiles),
        grid=(B, tiles),
        in_specs=in_specs,
        out_specs=x_spec,
        out_shape=jax.ShapeDtypeStruct(x.shape, F32),
        scratch_shapes=[
            pltpu.VMEM((D_INNER // LANES, TILE + 2 * HALO, LANES), F32),
            pltpu.VMEM((TILE, D_INNER), F32),
            pltpu.VMEM((TILE, D_INNER), F32),
        ],
        compiler_params=pltpu.CompilerParams(
            dimension_semantics=("arbitrary", "arbitrary"), vmem_limit_bytes=VMEM_LIMIT),
        name="conv_sample" if halo else "conv_prompt",
    )(*args)


def _decay_mask(diff, lgf, lgb):
    e = jnp.exp(jnp.abs(diff) * jnp.where(diff >= 0.0, lgf, lgb))
    return e * jnp.where(diff == 0.0, 2.0 * RET_DK ** -0.5, RET_DK ** -0.5)


def _head_norm_gate(o, gn, g):
    mu = jnp.mean(o, axis=-1, keepdims=True)
    d = o - mu
    var = jnp.mean(d * d, axis=-1, keepdims=True)
    return (d * lax.rsqrt(var + EPS) * gn * _silu(g)).astype(BF16)


def _ret_prompt_kernel(*refs, last):
    if last:
        (x_ref, mod_ref, ng_ref, win_ref, ld_ref, gn_ref, wout_ref, fg_ref, _st_in,
         o_ref, st_ref, m_ref, w_ref, z_ref) = refs
    else:
        (x_ref, mod_ref, ng_ref, win_ref, ld_ref, gn_ref, wout_ref,
         o_ref, st_ref, m_ref, w_ref, z_ref) = refs
    L = TILE

    @pl.when(pl.program_id(0) == 0)
    def _():
        lg = -jnp.exp(ld_ref[...])
        ii = lax.broadcasted_iota(jnp.int32, (L, L), 0)
        jj = lax.broadcasted_iota(jnp.int32, (L, L), 1)
        diff = (ii - jj).astype(F32)
        pos = lax.broadcasted_iota(jnp.int32, (1, L), 1).astype(F32)
        for hd in range(RET_HEADS):
            lgf = lg[0:1, hd:hd + 1]
            lgb = lg[1:2, hd:hd + 1]
            m_ref[hd] = _decay_mask(diff, lgf, lgb)
            w_ref[2 * hd:2 * hd + 1, :] = jnp.exp((L - 1.0 - pos) * lgf) * RET_DK ** -0.5
            w_ref[2 * hd + 1:2 * hd + 2, :] = jnp.exp(pos * lgb) * RET_DK ** -0.5

    x = x_ref[0]
    h = _mod_norm(x, ng_ref[...], mod_ref).astype(BF16)
    q = _dot(h, win_ref[:, 0:D_MODEL]).astype(BF16)
    k = _dot(h, win_ref[:, D_MODEL:2 * D_MODEL])
    v = _dot(h, win_ref[:, 2 * D_MODEL:2 * D_MODEL + D_INNER]).astype(BF16)
    g = _dot(h, win_ref[:, 2 * D_MODEL + D_INNER:])
    kb = k.astype(BF16)
    for hd in range(RET_HEADS):
        kcols = slice(hd * RET_DK, (hd + 1) * RET_DK)
        vcols = slice(hd * RET_DV, (hd + 1) * RET_DV)
        vh = v[:, vcols]
        s = _dot_nt(q[:, kcols], kb[:, kcols])
        p = (s * m_ref[hd]).astype(BF16)
        o = _dot(p, vh)
        z_ref[:, vcols] = _head_norm_gate(o, gn_ref[0:1, vcols], g[:, vcols])
        kt = k[:, kcols].T
        st_ref[0, 0, 0, hd] = _dot((kt * w_ref[2 * hd:2 * hd + 1, :]).astype(BF16), vh)
        st_ref[0, 0, 1, hd] = _dot((kt * w_ref[2 * hd + 1:2 * hd + 2, :]).astype(BF16), vh)
    xn = x + mod_ref[0, 2:3, :] * _dot(z_ref[...], wout_ref[...])
    if last:
        xn = _rms(xn, fg_ref[...])
    o_ref[0] = xn


def _ret_prompt_layer(x, mods, ng, w_in, ld, gn, w_out, final_g, states, *, i):
    B, L, _ = x.shape
    j = i // 2
    last = final_g is not None
    x_spec = pl.BlockSpec((1, L, D_MODEL), lambda b: (b, 0, 0))
    st_shape = (B, DEPTH // 2, 2, RET_HEADS, RET_DK, RET_DV)
    in_specs = [
        x_spec,
        pl.BlockSpec((None, 1, 3, D_MODEL), lambda b: (i, 0, 0, 0)),
        _const_spec((1, D_MODEL), i),
        _const_spec((D_MODEL, 2 * D_MODEL + 2 * D_INNER), j),
        _const_spec((2, RET_HEADS), j),
        _const_spec((1, D_INNER), j),
        _const_spec((D_INNER, D_MODEL), j),
    ]
    args = [x, mods, ng, w_in, ld, gn, w_out]
    aliases = {}
    if last:
        in_specs += [_const_spec((1, D_MODEL)), pl.BlockSpec(memory_space=pl.ANY)]
        args += [final_g, states]
        aliases = {len(args) - 1: 1}
    return pl.pallas_call(
        functools.partial(_ret_prompt_kernel, last=last),
        grid=(B,),
        in_specs=in_specs,
        out_specs=[x_spec,
                   pl.BlockSpec((1, 1, 2, RET_HEADS, RET_DK, RET_DV), lambda b: (b, j, 0, 0, 0, 0))],
        out_shape=[jax.ShapeDtypeStruct(x.shape, F32), jax.ShapeDtypeStruct(st_shape, F32)],
        scratch_shapes=[
            pltpu.VMEM((RET_HEADS, L, L), F32),
            pltpu.VMEM((2 * RET_HEADS, L), F32),
            pltpu.VMEM((L, D_INNER), BF16),
        ],
        input_output_aliases=aliases,
        compiler_params=pltpu.CompilerParams(
            dimension_semantics=("arbitrary",), vmem_limit_bytes=VMEM_LIMIT),
        name="ret_prompt",
    )(*args)


def _ret_sample_kernel(*refs, last):
    if last:
        (x_ref, mod_ref, ng_ref, wq_ref, wk_ref, wv_ref, wg_ref, ld_ref, gn_ref, wout_ref, s0_ref,
         cos_ref, sin_ref, fg_ref, o_ref, h_ref) = refs
    else:
        (x_ref, mod_ref, ng_ref, wq_ref, wk_ref, wv_ref, wg_ref, ld_ref, gn_ref, wout_ref, s0_ref,
         cos_ref, sin_ref, o_ref, h_ref) = refs
    L = x_ref.shape[1]
    half = RET_DK // 2
    hd = pl.program_id(1)

    @pl.when(hd == 0)
    def _():
        x = x_ref[0]
        h_ref[...] = _mod_norm(x, ng_ref[...], mod_ref).astype(BF16)
        o_ref[0] = x

    h = h_ref[...]
    lgf = -jnp.exp(jnp.full((1, L), ld_ref[0, hd], F32))
    lgb = -jnp.exp(jnp.full((1, L), ld_ref[1, hd], F32))
    cos = cos_ref[...]
    sin = sin_ref[...]

    def rope(t):
        t1, t2 = t[:, :half], t[:, half:]
        return jnp.concatenate([t1 * cos - t2 * sin, t2 * cos + t1 * sin], axis=-1)

    qh = rope(_dot(h, wq_ref[...]))
    kh = rope(_dot(h, wk_ref[...])).astype(BF16)
    vh = _dot(h, wv_ref[...]).astype(BF16)
    gh = _dot(h, wg_ref[...])
    posq = lax.broadcasted_iota(jnp.int32, (L, RET_DK), 0).astype(F32)
    lgf_q = -jnp.exp(jnp.full((1, RET_DK), ld_ref[0, hd], F32))
    lgb_q = -jnp.exp(jnp.full((1, RET_DK), ld_ref[1, hd], F32))
    ox = (_dot((qh * jnp.exp((posq + 1.0) * lgf_q)).astype(BF16), s0_ref[0, 0, 0, 0].astype(BF16))
          + _dot((qh * jnp.exp((L - posq) * lgb_q)).astype(BF16), s0_ref[0, 0, 1, 0].astype(BF16)))
    qb = qh.astype(BF16)
    gate = mod_ref[0, 2:3, :]
    for r in range(L // TILE):
        rows = slice(r * TILE, (r + 1) * TILE)
        ii = lax.broadcasted_iota(jnp.int32, (TILE, L), 0) + r * TILE
        jj = lax.broadcasted_iota(jnp.int32, (TILE, L), 1)
        m = _decay_mask((ii - jj).astype(F32), lgf, lgb)
        p = (_dot_nt(qb[rows], kh) * m).astype(BF16)
        o = _dot(p, vh) + ox[rows]
        z = _head_norm_gate(o, gn_ref[...], gh[rows])
        o_ref[0, rows, :] += gate * _dot(z, wout_ref[...])

    if last:
        @pl.when(hd == RET_HEADS - 1)
        def _():
            o_ref[0] = _rms(o_ref[0], fg_ref[...])


def _ret_sample_layer(x, mods, ng, w_in, ld, gn, w_out, state_ret, cos, sin, final_g, *, i):
    B, L, _ = x.shape
    j = i // 2
    last = final_g is not None
    x_spec = pl.BlockSpec((1, L, D_MODEL), lambda b, hd: (b, 0, 0))
    kb = D_MODEL // RET_DK
    vb = 2 * D_MODEL // RET_DV
    in_specs = [
        x_spec,
        pl.BlockSpec((None, 1, 3, D_MODEL), lambda b, hd: (i, 1 + b, 0, 0)),
        _const_spec((1, D_MODEL), i),
        pl.BlockSpec((None, D_MODEL, RET_DK), lambda b, hd: (j, 0, hd)),
        pl.BlockSpec((None, D_MODEL, RET_DK), lambda b, hd: (j, 0, kb + hd)),
        pl.BlockSpec((None, D_MODEL, RET_DV), lambda b, hd: (j, 0, vb + hd)),
        pl.BlockSpec((None, D_MODEL, RET_DV), lambda b, hd: (j, 0, vb + RET_HEADS + hd)),
        pl.BlockSpec(memory_space=pltpu.SMEM),
        pl.BlockSpec((None, 1, RET_DV), lambda b, hd: (j, 0, hd)),
        pl.BlockSpec((None, RET_DV, D_MODEL), lambda b, hd: (j, hd, 0)),
        pl.BlockSpec((1, 1, 2, 1, RET_DK, RET_DV), lambda b, hd: (b, j, 0, hd, 0, 0)),
        _const_spec((L, RET_DK // 2)),
        _const_spec((L, RET_DK // 2)),
    ]
    args = [x, mods, ng, w_in, w_in, w_in, w_in, ld[j], gn, w_out, state_ret, cos, sin]
    if last:
        in_specs.append(_const_spec((1, D_MODEL)))
        args.append(final_g)
    return pl.pallas_call(
        functools.partial(_ret_sample_kernel, last=last),
        grid=(B, RET_HEADS),
        in_specs=in_specs,
        out_specs=x_spec,
        out_shape=jax.ShapeDtypeStruct(x.shape, F32),
        scratch_shapes=[pltpu.VMEM((L, D_MODEL), BF16)],
        compiler_params=pltpu.CompilerParams(
            dimension_semantics=("arbitrary", "arbitrary"), vmem_limit_bytes=VMEM_LIMIT),
        name="ret_sample",
    )(*args)


def _rope_tables(L):
    rows = L // GRID_W
    r = jnp.repeat(jnp.arange(rows, dtype=F32), GRID_W)
    col = jnp.tile(jnp.arange(GRID_W, dtype=F32), rows)
    nf = RET_DK // 4
    inv = ROPE_BASE ** (-jnp.arange(nf, dtype=F32) / nf)
    ang = jnp.concatenate([r[:, None] * inv, col[:, None] * inv], axis=-1)
    return jnp.cos(ang), jnp.sin(ang)


def kernel(x_prompt, x_sample, c, state_ret, c_ctx, ada_w, ada_b, norm_g, final_norm_g, conv_w_in, conv_dw, conv_dw_b, conv_ln_g, conv_ln_b, conv_w_out, ret_w_in, ret_log_decay, ret_gn_g, ret_w_out):
    n_dec = c.shape[0]
    cond8 = jnp.concatenate([c_ctx[None, :], c, jnp.zeros((8 - 1 - n_dec, D_MODEL), F32)], axis=0)
    mods = _adaln(cond8, ada_w, ada_b).reshape(DEPTH, 8, 3, D_MODEL)
    cos, sin = _rope_tables(x_sample.shape[1])
    fg = final_norm_g.reshape(1, D_MODEL)

    ng = norm_g.reshape(DEPTH, 1, D_MODEL)
    conv_args = (ng, conv_w_in.astype(BF16), conv_dw, conv_dw_b[:, None, :], conv_ln_g[:, None, :],
                 conv_ln_b[:, None, :], conv_w_out.astype(BF16))
    ret_w_in_b = ret_w_in.astype(BF16)
    ret_w_out_b = ret_w_out.astype(BF16)
    gn = ret_gn_g[:, None, :]

    xp, xs = x_prompt, x_sample
    states = None
    for i in range(DEPTH):
        if i % 2 == 0:
            xp = _conv_layer(xp, mods, *conv_args, i=i, mod_row=0)
            xs = _conv_layer(xs, mods, *conv_args, i=i, mod_row=1)
        else:
            final_g = fg if i == DEPTH - 1 else None
            xp, states = _ret_prompt_layer(xp, mods, ng, ret_w_in_b, ret_log_decay, gn, ret_w_out_b, final_g,
                                           states, i=i)
            xs = _ret_sample_layer(xs, mods, ng, ret_w_in_b, ret_log_decay, gn, ret_w_out_b, state_ret, cos, sin,
                                   final_g, i=i)
    return (xp, xs, states)
```

```python
import functools

import jax
import jax.numpy as jnp
from jax import lax
from jax.experimental import pallas as pl
from jax.experimental.pallas import tpu as pltpu

D_MODEL = 1024
D_INNER = 2048
DEPTH = 4
CONV_K = 31
HALO = 16
RET_HEADS = 4
RET_DK = 256
RET_DV = 512
GRID_W = 64
ROPE_BASE = 10000.0
EPS = 1e-6
TILE = 256
LANES = 128
CHUNK = 512
VMEM_LIMIT = 56 * 1024 * 1024

F32 = jnp.float32
BF16 = jnp.bfloat16


def _dot(a, b):
    return jnp.dot(a, b, preferred_element_type=F32)


def _dot_nt(a, b):
    return lax.dot_general(a, b, (((1,), (1,)), ((), ())), preferred_element_type=F32)


def _sigmoid(x):
    return 1.0 / (1.0 + jnp.exp(-x))


def _silu(x):
    return x * _sigmoid(x)


def _rms(x, g):
    return x * lax.rsqrt(jnp.mean(x * x, axis=-1, keepdims=True) + EPS) * g


def _mod_norm(x, g, mod_ref):
    return _rms(x, g) * (1.0 + mod_ref[0, 1:2, :]) + mod_ref[0, 0:1, :]


def _adaln_kernel(cond_ref, w_ref, b_ref, o_ref):
    s = _silu(cond_ref[...]).astype(BF16)
    o_ref[0] = _dot(s, w_ref[0].astype(BF16)) + b_ref[0]


def _adaln(cond8, ada_w, ada_b):
    nb = 3 * D_MODEL // D_MODEL
    return pl.pallas_call(
        _adaln_kernel,
        grid=(DEPTH, nb),
        in_specs=[
            pl.BlockSpec((8, D_MODEL), lambda i, n: (0, 0)),
            pl.BlockSpec((1, D_MODEL, D_MODEL), lambda i, n: (i, 0, n)),
            pl.BlockSpec((1, 1, D_MODEL), lambda i, n: (i, 0, n)),
        ],
        out_specs=pl.BlockSpec((1, 8, D_MODEL), lambda i, n: (i, 0, n)),
        out_shape=jax.ShapeDtypeStruct((DEPTH, 8, 3 * D_MODEL), F32),
        compiler_params=pltpu.CompilerParams(
            dimension_semantics=("arbitrary", "arbitrary"), vmem_limit_bytes=VMEM_LIMIT),
        name="adaln",
    )(cond8, ada_w, ada_b.reshape(DEPTH, 1, 3 * D_MODEL))


def _conv_kernel(*refs, halo, tiles_per_seq):
    if halo:
        (x_ref, xt_ref, xb_ref, mod_ref, ng_ref, win_ref, dw_ref, dwb_ref, lng_ref, lnb_ref,
         wout_ref, o_ref, pad_ref, u_ref, gate_ref) = refs
    else:
        (x_ref, mod_ref, ng_ref, win_ref, dw_ref, dwb_ref, lng_ref, lnb_ref,
         wout_ref, o_ref, pad_ref, u_ref, gate_ref) = refs
    n_slab = CHUNK // LANES
    ng = ng_ref[...]
    h = _mod_norm(x_ref[0], ng, mod_ref).astype(BF16)
    if halo:
        t = pl.program_id(1)
        ht = _mod_norm(xt_ref[0], ng, mod_ref).astype(BF16)
        hb = _mod_norm(xb_ref[0], ng, mod_ref).astype(BF16)
        hext = jnp.concatenate([ht, h, hb], axis=0)
        row = lax.broadcasted_iota(jnp.int32, (TILE + 2 * HALO, CHUNK), 0)
        valid = jnp.logical_and(jnp.logical_or(row >= HALO, t > 0),
                                jnp.logical_or(row < HALO + TILE, t < tiles_per_seq - 1))
    else:
        hext = h
        zeros = jnp.zeros((HALO, LANES), F32)
        for s in range(D_INNER // LANES):
            pad_ref[s, 0:HALO, :] = zeros
            pad_ref[s, HALO + TILE:2 * HALO + TILE, :] = zeros

    for c in range(D_INNER // CHUNK):
        c0 = c * CHUNK
        a = _dot(hext, win_ref[:, c0:c0 + CHUNK])
        b = _dot(hext, win_ref[:, D_INNER + c0:D_INNER + c0 + CHUNK])
        g = _dot(h, win_ref[:, 2 * D_INNER + c0:2 * D_INNER + c0 + CHUNK])
        u = a * _sigmoid(b)
        gate_ref[:, c0:c0 + CHUNK] = _silu(g.astype(BF16))
        if halo:
            u = jnp.where(valid, u, 0.0)
        for l in range(n_slab):
            s = c * n_slab + l
            if halo:
                pad_ref[s, :, :] = u[:, l * LANES:(l + 1) * LANES]
            else:
                pad_ref[s, HALO:HALO + TILE, :] = u[:, l * LANES:(l + 1) * LANES]
        for l in range(n_slab):
            s = c * n_slab + l
            lanes = slice(c0 + l * LANES, c0 + (l + 1) * LANES)
            for rb in range(TILE // LANES):
                acc = jnp.broadcast_to(dwb_ref[0:1, lanes], (LANES, LANES))
                for k in range(CONV_K):
                    start = rb * LANES + k + HALO - CONV_K // 2
                    acc = acc + pad_ref[s, pl.ds(start, LANES), :] * dw_ref[k:k + 1, lanes]
                u_ref[rb * LANES:(rb + 1) * LANES, lanes] = acc

    u = u_ref[...]
    mu = jnp.mean(u, axis=-1, keepdims=True)
    d = u - mu
    var = jnp.mean(d * d, axis=-1, keepdims=True)
    y = d * lax.rsqrt(var + EPS) * lng_ref[...] + lnb_ref[...]
    z = _silu(y.astype(BF16)) * gate_ref[...]
    out = _dot(z, wout_ref[...])
    o_ref[0] = x_ref[0] + mod_ref[0, 2:3, :] * out


def _const_spec(shape, layer=None):
    nd = len(shape)
    if layer is None:
        return pl.BlockSpec(shape, lambda *_: (0,) * nd, pipeline_mode=pl.Buffered(1))
    return pl.BlockSpec((None,) + shape, lambda *_: (layer,) + (0,) * nd, pipeline_mode=pl.Buffered(1))


def _conv_layer(x, mods, ng, w_in, dw, dwb, lng, lnb, w_out, *, i, mod_row):
    B, L, _ = x.shape
    j = i // 2
    tiles = L // TILE
    halo = tiles > 1
    hb = TILE // HALO
    x_spec = pl.BlockSpec((1, TILE, D_MODEL), lambda b, t: (b, t, 0))
    in_specs = [x_spec]
    args = [x]
    if halo:
        in_specs += [
            pl.BlockSpec((1, HALO, D_MODEL), lambda b, t: (b, jnp.maximum(t * hb - 1, 0), 0)),
            pl.BlockSpec((1, HALO, D_MODEL), lambda b, t: (b, jnp.minimum((t + 1) * hb, L // HALO - 1), 0)),
        ]
        args += [x, x]
    mod_map = (lambda b, t: (i, mod_row + b, 0, 0)) if mod_row else (lambda b, t: (i, 0, 0, 0))
    in_specs += [
        pl.BlockSpec((None, 1, 3, D_MODEL), mod_map),
        _const_spec((1, D_MODEL), i),
        _const_spec((D_MODEL, 3 * D_INNER), j),
        _const_spec((CONV_K, D_INNER), j),
        _const_spec((1, D_INNER), j),
        _const_spec((1, D_INNER), j),
        _const_spec((1, D_INNER), j),
        _const_spec((D_INNER, D_MODEL), j),
    ]
    args += [mods, ng, w_in, dw, dwb, lng, lnb, w_out]
    return pl.pallas_call(
        functools.partial(_conv_kernel, halo=halo, tiles_per_seq=tiles),
        grid=(B, tiles),
        in_specs=in_specs,
        out_specs=x_spec,
        out_shape=jax.ShapeDtypeStruct(x.shape, F32),
        scratch_shapes=[
            pltpu.VMEM((D_INNER // LANES, TILE + 2 * HALO, LANES), F32),
            pltpu.VMEM((TILE, D_INNER), F32),
            pltpu.VMEM((TILE, D_INNER), BF16),
        ],
        compiler_params=pltpu.CompilerParams(
            dimension_semantics=("arbitrary", "arbitrary"), vmem_limit_bytes=VMEM_LIMIT),
        name="conv_sample" if halo else "conv_prompt",
    )(*args)


def _decay_mask(diff, lgf, lgb):
    e = jnp.exp(jnp.abs(diff) * jnp.where(diff >= 0.0, lgf, lgb))
    return e * jnp.where(diff == 0.0, 2.0 * RET_DK ** -0.5, RET_DK ** -0.5)


def _head_norm_gate(o, gn, g):
    mu = jnp.mean(o, axis=-1, keepdims=True)
    d = o - mu
    var = jnp.mean(d * d, axis=-1, keepdims=True)
    return (d * lax.rsqrt(var + EPS) * gn * _silu(g)).astype(BF16)


def _ret_prompt_kernel(*refs, last):
    if last:
        (x_ref, mod_ref, ng_ref, win_ref, ld_ref, gn_ref, wout_ref, fg_ref, _st_in,
         o_ref, st_ref, m_ref, w_ref, z_ref) = refs
    else:
        (x_ref, mod_ref, ng_ref, win_ref, ld_ref, gn_ref, wout_ref,
         o_ref, st_ref, m_ref, w_ref, z_ref) = refs
    L = TILE

    @pl.when(pl.program_id(0) == 0)
    def _():
        lg = -jnp.exp(ld_ref[...])
        ii = lax.broadcasted_iota(jnp.int32, (L, L), 0)
        jj = lax.broadcasted_iota(jnp.int32, (L, L), 1)
        diff = (ii - jj).astype(F32)
        pos = lax.broadcasted_iota(jnp.int32, (1, L), 1).astype(F32)
        for hd in range(RET_HEADS):
            lgf = lg[0:1, hd:hd + 1]
            lgb = lg[1:2, hd:hd + 1]
            m_ref[hd] = _decay_mask(diff, lgf, lgb)
            w_ref[2 * hd:2 * hd + 1, :] = jnp.exp((L - 1.0 - pos) * lgf) * RET_DK ** -0.5
            w_ref[2 * hd + 1:2 * hd + 2, :] = jnp.exp(pos * lgb) * RET_DK ** -0.5

    x = x_ref[0]
    h = _mod_norm(x, ng_ref[...], mod_ref).astype(BF16)
    q = _dot(h, win_ref[:, 0:D_MODEL].astype(BF16)).astype(BF16)
    k = _dot(h, win_ref[:, D_MODEL:2 * D_MODEL].astype(BF16))
    v = _dot(h, win_ref[:, 2 * D_MODEL:2 * D_MODEL + D_INNER].astype(BF16)).astype(BF16)
    g = _dot(h, win_ref[:, 2 * D_MODEL + D_INNER:].astype(BF16))
    kb = k.astype(BF16)
    for hd in range(RET_HEADS):
        kcols = slice(hd * RET_DK, (hd + 1) * RET_DK)
        vcols = slice(hd * RET_DV, (hd + 1) * RET_DV)
        vh = v[:, vcols]
        s = _dot_nt(q[:, kcols], kb[:, kcols])
        p = (s * m_ref[hd]).astype(BF16)
        o = _dot(p, vh)
        z_ref[:, vcols] = _head_norm_gate(o, gn_ref[0:1, vcols], g[:, vcols])
        kt = k[:, kcols].T
        st_ref[0, 0, 0, hd] = _dot((kt * w_ref[2 * hd:2 * hd + 1, :]).astype(BF16), vh)
        st_ref[0, 0, 1, hd] = _dot((kt * w_ref[2 * hd + 1:2 * hd + 2, :]).astype(BF16), vh)
    xn = x + mod_ref[0, 2:3, :] * _dot(z_ref[...], wout_ref[...].astype(BF16))
    if last:
        xn = _rms(xn, fg_ref[...])
    o_ref[0] = xn


def _ret_prompt_layer(x, mods, ng, w_in, ld, gn, w_out, final_g, states, *, i):
    B, L, _ = x.shape
    j = i // 2
    last = final_g is not None
    x_spec = pl.BlockSpec((1, L, D_MODEL), lambda b: (b, 0, 0))
    st_shape = (B, DEPTH // 2, 2, RET_HEADS, RET_DK, RET_DV)
    in_specs = [
        x_spec,
        pl.BlockSpec((None, 1, 3, D_MODEL), lambda b: (i, 0, 0, 0)),
        _const_spec((1, D_MODEL), i),
        _const_spec((D_MODEL, 2 * D_MODEL + 2 * D_INNER), j),
        _const_spec((2, RET_HEADS), j),
        _const_spec((1, D_INNER), j),
        _const_spec((D_INNER, D_MODEL), j),
    ]
    args = [x, mods, ng, w_in, ld, gn, w_out]
    aliases = {}
    if last:
        in_specs += [_const_spec((1, D_MODEL)), pl.BlockSpec(memory_space=pl.ANY)]
        args += [final_g, states]
        aliases = {len(args) - 1: 1}
    return pl.pallas_call(
        functools.partial(_ret_prompt_kernel, last=last),
        grid=(B,),
        in_specs=in_specs,
        out_specs=[x_spec,
                   pl.BlockSpec((1, 1, 2, RET_HEADS, RET_DK, RET_DV), lambda b: (b, j, 0, 0, 0, 0))],
        out_shape=[jax.ShapeDtypeStruct(x.shape, F32), jax.ShapeDtypeStruct(st_shape, F32)],
        scratch_shapes=[
            pltpu.VMEM((RET_HEADS, L, L), F32),
            pltpu.VMEM((2 * RET_HEADS, L), F32),
            pltpu.VMEM((L, D_INNER), BF16),
        ],
        input_output_aliases=aliases,
        compiler_params=pltpu.CompilerParams(
            dimension_semantics=("arbitrary",), vmem_limit_bytes=VMEM_LIMIT),
        name="ret_prompt",
    )(*args)


def _ret_sample_kernel(*refs, last):
    if last:
        (x_ref, mod_ref, ng_ref, wq_ref, wk_ref, wv_ref, wg_ref, ld_ref, gn_ref, wout_ref, s0_ref,
         cos_ref, sin_ref, fg_ref, o_ref, h_ref) = refs
    else:
        (x_ref, mod_ref, ng_ref, wq_ref, wk_ref, wv_ref, wg_ref, ld_ref, gn_ref, wout_ref, s0_ref,
         cos_ref, sin_ref, o_ref, h_ref) = refs
    L = x_ref.shape[1]
    half = RET_DK // 2
    hd = pl.program_id(1)

    @pl.when(hd == 0)
    def _():
        x = x_ref[0]
        h_ref[...] = _mod_norm(x, ng_ref[...], mod_ref).astype(BF16)
        o_ref[0] = x

    h = h_ref[...]
    cos = cos_ref[...]
    sin = sin_ref[...]

    def rope(t):
        t1, t2 = t[:, :half], t[:, half:]
        return jnp.concatenate([t1 * cos - t2 * sin, t2 * cos + t1 * sin], axis=-1)

    qh = rope(_dot(h, wq_ref[...].astype(BF16)))
    kh = rope(_dot(h, wk_ref[...].astype(BF16))).astype(BF16)
    vh = _dot(h, wv_ref[...].astype(BF16)).astype(BF16)
    gh = _dot(h, wg_ref[...].astype(BF16))
    wout = wout_ref[...].astype(BF16)
    posq = lax.broadcasted_iota(jnp.int32, (L, RET_DK), 0).astype(F32)
    lgf_q = -jnp.exp(jnp.full((1, RET_DK), ld_ref[0, hd], F32))
    lgb_q = -jnp.exp(jnp.full((1, RET_DK), ld_ref[1, hd], F32))
    ox = (_dot((qh * jnp.exp((posq + 1.0) * lgf_q)).astype(BF16), s0_ref[0, 0, 0, 0].astype(BF16))
          + _dot((qh * jnp.exp((L - posq) * lgb_q)).astype(BF16), s0_ref[0, 0, 1, 0].astype(BF16)))
    qb = qh.astype(BF16)
    gate = mod_ref[0, 2:3, :]
    n_blk = L // TILE
    ii = lax.broadcasted_iota(jnp.int32, (TILE, TILE), 0)
    jj = lax.broadcasted_iota(jnp.int32, (TILE, TILE), 1)
    diff = (ii - jj).astype(F32)
    m_diag = _decay_mask(diff, lgf_q, lgb_q)
    below = [jnp.exp((TILE + diff) * lgf_q) * RET_DK ** -0.5]
    above = [jnp.exp((TILE - diff) * lgb_q) * RET_DK ** -0.5]
    for dist in range(1, n_blk - 1):
        below.append(below[0] * jnp.exp((TILE * dist) * lgf_q))
        above.append(above[0] * jnp.exp((TILE * dist) * lgb_q))
    for r in range(n_blk):
        rows = slice(r * TILE, (r + 1) * TILE)
        m = jnp.concatenate([m_diag if cb == r else below[r - cb - 1] if cb < r else above[cb - r - 1]
                             for cb in range(n_blk)], axis=1)
        p = (_dot_nt(qb[rows], kh) * m).astype(BF16)
        o = _dot(p, vh) + ox[rows]
        z = _head_norm_gate(o, gn_ref[...], gh[rows])
        o_ref[0, rows, :] += gate * _dot(z, wout)

    if last:
        @pl.when(hd == RET_HEADS - 1)
        def _():
            o_ref[0] = _rms(o_ref[0], fg_ref[...])


def _ret_sample_layer(x, mods, ng, w_in, ld, gn, w_out, state_ret, cos, sin, final_g, *, i):
    B, L, _ = x.shape
    j = i // 2
    last = final_g is not None
    x_spec = pl.BlockSpec((1, L, D_MODEL), lambda b, hd: (b, 0, 0))
    kb = D_MODEL // RET_DK
    vb = 2 * D_MODEL // RET_DV
    in_specs = [
        x_spec,
        pl.BlockSpec((None, 1, 3, D_MODEL), lambda b, hd: (i, 1 + b, 0, 0)),
        _const_spec((1, D_MODEL), i),
        pl.BlockSpec((None, D_MODEL, RET_DK), lambda b, hd: (j, 0, hd)),
        pl.BlockSpec((None, D_MODEL, RET_DK), lambda b, hd: (j, 0, kb + hd)),
        pl.BlockSpec((None, D_MODEL, RET_DV), lambda b, hd: (j, 0, vb + hd)),
        pl.BlockSpec((None, D_MODEL, RET_DV), lambda b, hd: (j, 0, vb + RET_HEADS + hd)),
        pl.BlockSpec(memory_space=pltpu.SMEM),
        pl.BlockSpec((None, 1, RET_DV), lambda b, hd: (j, 0, hd)),
        pl.BlockSpec((None, RET_DV, D_MODEL), lambda b, hd: (j, hd, 0)),
        pl.BlockSpec((1, 1, 2, 1, RET_DK, RET_DV), lambda b, hd: (b, j, 0, hd, 0, 0)),
        _const_spec((L, RET_DK // 2)),
        _const_spec((L, RET_DK // 2)),
    ]
    args = [x, mods, ng, w_in, w_in, w_in, w_in, ld[j], gn, w_out, state_ret, cos, sin]
    if last:
        in_specs.append(_const_spec((1, D_MODEL)))
        args.append(final_g)
    return pl.pallas_call(
        functools.partial(_ret_sample_kernel, last=last),
        grid=(B, RET_HEADS),
        in_specs=in_specs,
        out_specs=x_spec,
        out_shape=jax.ShapeDtypeStruct(x.shape, F32),
        scratch_shapes=[pltpu.VMEM((L, D_MODEL), BF16)],
        compiler_params=pltpu.CompilerParams(
            dimension_semantics=("arbitrary", "arbitrary"), vmem_limit_bytes=VMEM_LIMIT),
        name="ret_sample",
    )(*args)


def _rope_tables(L):
    rows = L // GRID_W
    r = jnp.repeat(jnp.arange(rows, dtype=F32), GRID_W)
    col = jnp.tile(jnp.arange(GRID_W, dtype=F32), rows)
    nf = RET_DK // 4
    inv = ROPE_BASE ** (-jnp.arange(nf, dtype=F32) / nf)
    ang = jnp.concatenate([r[:, None] * inv, col[:, None] * inv], axis=-1)
    return jnp.cos(ang), jnp.sin(ang)


def kernel(x_prompt, x_sample, c, state_ret, c_ctx, ada_w, ada_b, norm_g, final_norm_g, conv_w_in, conv_dw, conv_dw_b, conv_ln_g, conv_ln_b, conv_w_out, ret_w_in, ret_log_decay, ret_gn_g, ret_w_out):
    n_dec = c.shape[0]
    cond8 = jnp.concatenate([c_ctx[None, :], c, jnp.zeros((8 - 1 - n_dec, D_MODEL), F32)], axis=0)
    mods = _adaln(cond8, ada_w, ada_b).reshape(DEPTH, 8, 3, D_MODEL)
    cos, sin = _rope_tables(x_sample.shape[1])
    fg = final_norm_g.reshape(1, D_MODEL)

    ng = norm_g.reshape(DEPTH, 1, D_MODEL)
    conv_args = (ng, conv_w_in.astype(BF16), conv_dw, conv_dw_b[:, None, :], conv_ln_g[:, None, :],
                 conv_ln_b[:, None, :], conv_w_out.astype(BF16))
    gn = ret_gn_g[:, None, :]

    xp, xs = x_prompt, x_sample
    states = None
    for i in range(DEPTH):
        if i % 2 == 0:
            xp = _conv_layer(xp, mods, *conv_args, i=i, mod_row=0)
            xs = _conv_layer(xs, mods, *conv_args, i=i, mod_row=1)
        else:
            final_g = fg if i == DEPTH - 1 else None
            xp, states = _ret_prompt_layer(xp, mods, ng, ret_w_in, ret_log_decay, gn, ret_w_out, final_g,
                                           states, i=i)
            xs = _ret_sample_layer(xs, mods, ng, ret_w_in, ret_log_decay, gn, ret_w_out, state_ret, cos, sin,
                                   final_g, i=i)
    return (xp, xs, states)
```

```python
import functools

import jax
import jax.numpy as jnp
from jax import lax
from jax.experimental import pallas as pl
from jax.experimental.pallas import tpu as pltpu

D_MODEL = 1024
D_INNER = 2048
DEPTH = 4
CONV_K = 31
HALO = 16
RET_HEADS = 4
RET_DK = 256
RET_DV = 512
GRID_W = 64
ROPE_BASE = 10000.0
EPS = 1e-6
TILE = 256
LANES = 128
CHUNK = 256
SEQS_PER_STEP = 2
VMEM_LIMIT = 56 * 1024 * 1024

F32 = jnp.float32
BF16 = jnp.bfloat16


def _dot(a, b):
    return jnp.dot(a, b, preferred_element_type=F32)


def _dot_nt(a, b):
    return lax.dot_general(a, b, (((1,), (1,)), ((), ())), preferred_element_type=F32)


def _sigmoid(x):
    return 1.0 / (1.0 + jnp.exp(-x))


def _silu(x):
    return x * _sigmoid(x)


def _rms(x, g):
    return x * lax.rsqrt(jnp.mean(x * x, axis=-1, keepdims=True) + EPS) * g


def _mod_norm(x, g, mod_ref):
    return _rms(x, g) * (1.0 + mod_ref[0, 1:2, :]) + mod_ref[0, 0:1, :]


def _adaln_kernel(cond_ref, w_ref, b_ref, o_ref):
    s = _silu(cond_ref[...]).astype(BF16)
    o_ref[0] = _dot(s, w_ref[0].astype(BF16)) + b_ref[0]


def _adaln(cond8, ada_w, ada_b):
    nb = 3 * D_MODEL // D_MODEL
    return pl.pallas_call(
        _adaln_kernel,
        grid=(DEPTH, nb),
        in_specs=[
            pl.BlockSpec((8, D_MODEL), lambda i, n: (0, 0)),
            pl.BlockSpec((1, D_MODEL, D_MODEL), lambda i, n: (i, 0, n)),
            pl.BlockSpec((1, 1, D_MODEL), lambda i, n: (i, 0, n)),
        ],
        out_specs=pl.BlockSpec((1, 8, D_MODEL), lambda i, n: (i, 0, n)),
        out_shape=jax.ShapeDtypeStruct((DEPTH, 8, 3 * D_MODEL), F32),
        compiler_params=pltpu.CompilerParams(
            dimension_semantics=("arbitrary", "arbitrary"), vmem_limit_bytes=VMEM_LIMIT),
        name="adaln",
    )(cond8, ada_w, ada_b.reshape(DEPTH, 1, 3 * D_MODEL))


def _exact_zero_after(v):
    bits = pltpu.bitcast(v, jnp.uint32)
    return pltpu.bitcast((bits >> 16) >> 16, F32)


def _conv_kernel(*refs, halo, tiles_per_seq):
    if halo:
        (x_ref, xt_ref, xb_ref, mod_ref, ng_ref, win_ref, dw_ref, dwb_ref, lng_ref, lnb_ref,
         wout_ref, o_ref, pad_ref, u_ref, gate_ref) = refs
    else:
        (x_ref, mod_ref, ng_ref, win_ref, dw_ref, dwb_ref, lng_ref, lnb_ref,
         wout_ref, o_ref, pad_ref, u_ref, gate_ref) = refs
    n_seq = x_ref.shape[0]
    n_slab = CHUNK // LANES
    n_chunk = D_INNER // CHUNK
    ng = ng_ref[...]
    h = _mod_norm(x_ref[...].reshape(n_seq * TILE, D_MODEL), ng, mod_ref).astype(BF16)
    if halo:
        t = pl.program_id(1)
        ht = _mod_norm(xt_ref[0], ng, mod_ref).astype(BF16)
        hb = _mod_norm(xb_ref[0], ng, mod_ref).astype(BF16)
        hext = jnp.concatenate([ht, h, hb], axis=0)
        row = lax.broadcasted_iota(jnp.int32, (TILE + 2 * HALO, CHUNK), 0)
        valid = jnp.logical_and(jnp.logical_or(row >= HALO, t > 0),
                                jnp.logical_or(row < HALO + TILE, t < tiles_per_seq - 1))
    else:
        hext = h
        zeros = jnp.zeros((HALO, LANES), F32)
        for s in range(D_INNER // LANES):
            for q in range(n_seq):
                pad_ref[s, q, 0:HALO, :] = zeros
                pad_ref[s, q, HALO + TILE:2 * HALO + TILE, :] = zeros

    def finish(q):
        rows = slice(q * TILE, (q + 1) * TILE)
        u = u_ref[rows, :]
        mu = jnp.mean(u, axis=-1, keepdims=True)
        d = u - mu
        var = jnp.mean(d * d, axis=-1, keepdims=True)
        y = d * lax.rsqrt(var + EPS) * lng_ref[...] + lnb_ref[...]
        z = _silu(y.astype(BF16)) * gate_ref[rows, :]
        out = x_ref[q] + mod_ref[0, 2:3, :] * _dot(z, wout_ref[...])
        o_ref[q] = out
        return out

    pin = None
    for c in range(n_chunk):
        c0 = c * CHUNK
        a = _dot(hext, win_ref[:, c0:c0 + CHUNK])
        b = _dot(hext, win_ref[:, D_INNER + c0:D_INNER + c0 + CHUNK])
        g = _dot(h, win_ref[:, 2 * D_INNER + c0:2 * D_INNER + c0 + CHUNK])
        u = a * _sigmoid(b)
        gate_ref[:, c0:c0 + CHUNK] = _silu(g.astype(BF16))
        if halo:
            u = jnp.where(valid, u, 0.0)
        for l in range(n_slab):
            s = c * n_slab + l
            if halo:
                pad_ref[s, 0, :, :] = u[:, l * LANES:(l + 1) * LANES]
            else:
                for q in range(n_seq):
                    pad_ref[s, q, HALO:HALO + TILE, :] = u[q * TILE:(q + 1) * TILE, l * LANES:(l + 1) * LANES]
        for q in range(n_seq):
            if q > 0 and c == n_chunk - 1:
                pin = _exact_zero_after(finish(q - 1)[TILE - 8:TILE, D_MODEL - LANES:D_MODEL])[0:1, :]
            for l in range(n_slab):
                s = c * n_slab + l
                lanes = slice(c0 + l * LANES, c0 + (l + 1) * LANES)
                bias = dwb_ref[0:1, lanes]
                if q > 0 and c == n_chunk - 1:
                    bias = bias + pin
                for rb in range(TILE // LANES):
                    acc = jnp.broadcast_to(bias, (LANES, LANES))
                    for k in range(CONV_K):
                        start = rb * LANES + k + HALO - CONV_K // 2
                        acc = acc + pad_ref[s, q, pl.ds(start, LANES), :] * dw_ref[k:k + 1, lanes]
                    u_ref[q * TILE + rb * LANES:q * TILE + (rb + 1) * LANES, lanes] = acc
    finish(n_seq - 1)


def _const_spec(shape, layer=None):
    nd = len(shape)
    if layer is None:
        return pl.BlockSpec(shape, lambda *_: (0,) * nd, pipeline_mode=pl.Buffered(1))
    return pl.BlockSpec((None,) + shape, lambda *_: (layer,) + (0,) * nd, pipeline_mode=pl.Buffered(1))


def _conv_layer(x, mods, ng, w_in, dw, dwb, lng, lnb, w_out, *, i, mod_row):
    B, L, _ = x.shape
    j = i // 2
    tiles = L // TILE
    halo = tiles > 1
    hb = TILE // HALO
    n_seq = 1 if halo else SEQS_PER_STEP
    x_spec = pl.BlockSpec((n_seq, TILE, D_MODEL), lambda b, t: (b, t, 0))
    in_specs = [x_spec]
    args = [x]
    if halo:
        in_specs += [
            pl.BlockSpec((1, HALO, D_MODEL), lambda b, t: (b, jnp.maximum(t * hb - 1, 0), 0)),
            pl.BlockSpec((1, HALO, D_MODEL), lambda b, t: (b, jnp.minimum((t + 1) * hb, L // HALO - 1), 0)),
        ]
        args += [x, x]
    mod_map = (lambda b, t: (i, mod_row + b, 0, 0)) if mod_row else (lambda b, t: (i, 0, 0, 0))
    in_specs += [
        pl.BlockSpec((None, 1, 3, D_MODEL), mod_map),
        _const_spec((1, D_MODEL), i),
        _const_spec((D_MODEL, 3 * D_INNER), j),
        _const_spec((CONV_K, D_INNER), j),
        _const_spec((1, D_INNER), j),
        _const_spec((1, D_INNER), j),
        _const_spec((1, D_INNER), j),
        _const_spec((D_INNER, D_MODEL), j),
    ]
    args += [mods, ng, w_in, dw, dwb, lng, lnb, w_out]
    return pl.pallas_call(
        functools.partial(_conv_kernel, halo=halo, tiles_per_seq=tiles),
        grid=(B // n_seq, tiles),
        in_specs=in_specs,
        out_specs=x_spec,
        out_shape=jax.ShapeDtypeStruct(x.shape, F32),
        scratch_shapes=[
            pltpu.VMEM((D_INNER // LANES, n_seq, TILE + 2 * HALO, LANES), F32),
            pltpu.VMEM((n_seq * TILE, D_INNER), F32),
            pltpu.VMEM((n_seq * TILE, D_INNER), BF16),
        ],
        compiler_params=pltpu.CompilerParams(
            dimension_semantics=("arbitrary", "arbitrary"), vmem_limit_bytes=VMEM_LIMIT),
        name="conv_sample" if halo else "conv_prompt",
    )(*args)


def _decay_mask(diff, lgf, lgb):
    e = jnp.exp(jnp.abs(diff) * jnp.where(diff >= 0.0, lgf, lgb))
    return e * jnp.where(diff == 0.0, 2.0 * RET_DK ** -0.5, RET_DK ** -0.5)


def _head_norm_gate(o, gn, g):
    mu = jnp.mean(o, axis=-1, keepdims=True)
    d = o - mu
    var = jnp.mean(d * d, axis=-1, keepdims=True)
    return (d * lax.rsqrt(var + EPS) * gn * _silu(g)).astype(BF16)


def _ret_prompt_kernel(*refs, last):
    if last:
        (x_ref, mod_ref, ng_ref, win_ref, ld_ref, gn_ref, wout_ref, fg_ref, _st_in,
         o_ref, st_ref, m_ref, w_ref, z_ref) = refs
    else:
        (x_ref, mod_ref, ng_ref, win_ref, ld_ref, gn_ref, wout_ref,
         o_ref, st_ref, m_ref, w_ref, z_ref) = refs
    L = TILE

    @pl.when(pl.program_id(0) == 0)
    def _():
        lg = -jnp.exp(ld_ref[...])
        ii = lax.broadcasted_iota(jnp.int32, (L, L), 0)
        jj = lax.broadcasted_iota(jnp.int32, (L, L), 1)
        diff = (ii - jj).astype(F32)
        pos = lax.broadcasted_iota(jnp.int32, (1, L), 1).astype(F32)
        for hd in range(RET_HEADS):
            lgf = lg[0:1, hd:hd + 1]
            lgb = lg[1:2, hd:hd + 1]
            m_ref[hd] = _decay_mask(diff, lgf, lgb)
            w_ref[2 * hd:2 * hd + 1, :] = jnp.exp((L - 1.0 - pos) * lgf) * RET_DK ** -0.5
            w_ref[2 * hd + 1:2 * hd + 2, :] = jnp.exp(pos * lgb) * RET_DK ** -0.5

    x = x_ref[0]
    h = _mod_norm(x, ng_ref[...], mod_ref).astype(BF16)
    q = _dot(h, win_ref[:, 0:D_MODEL].astype(BF16)).astype(BF16)
    k = _dot(h, win_ref[:, D_MODEL:2 * D_MODEL].astype(BF16))
    v = _dot(h, win_ref[:, 2 * D_MODEL:2 * D_MODEL + D_INNER].astype(BF16)).astype(BF16)
    g = _dot(h, win_ref[:, 2 * D_MODEL + D_INNER:].astype(BF16))
    kb = k.astype(BF16)
    for hd in range(RET_HEADS):
        kcols = slice(hd * RET_DK, (hd + 1) * RET_DK)
        vcols = slice(hd * RET_DV, (hd + 1) * RET_DV)
        vh = v[:, vcols]
        s = _dot_nt(q[:, kcols], kb[:, kcols])
        p = (s * m_ref[hd]).astype(BF16)
        o = _dot(p, vh)
        z_ref[:, vcols] = _head_norm_gate(o, gn_ref[0:1, vcols], g[:, vcols])
        kt = k[:, kcols].T
        st_ref[0, 0, 0, hd] = _dot((kt * w_ref[2 * hd:2 * hd + 1, :]).astype(BF16), vh)
        st_ref[0, 0, 1, hd] = _dot((kt * w_ref[2 * hd + 1:2 * hd + 2, :]).astype(BF16), vh)
    xn = x + mod_ref[0, 2:3, :] * _dot(z_ref[...], wout_ref[...].astype(BF16))
    if last:
        xn = _rms(xn, fg_ref[...])
    o_ref[0] = xn


def _ret_prompt_layer(x, mods, ng, w_in, ld, gn, w_out, final_g, states, *, i):
    B, L, _ = x.shape
    j = i // 2
    last = final_g is not None
    x_spec = pl.BlockSpec((1, L, D_MODEL), lambda b: (b, 0, 0))
    st_shape = (B, DEPTH // 2, 2, RET_HEADS, RET_DK, RET_DV)
    in_specs = [
        x_spec,
        pl.BlockSpec((None, 1, 3, D_MODEL), lambda b: (i, 0, 0, 0)),
        _const_spec((1, D_MODEL), i),
        _const_spec((D_MODEL, 2 * D_MODEL + 2 * D_INNER), j),
        _const_spec((2, RET_HEADS), j),
        _const_spec((1, D_INNER), j),
        _const_spec((D_INNER, D_MODEL), j),
    ]
    args = [x, mods, ng, w_in, ld, gn, w_out]
    aliases = {}
    if last:
        in_specs += [_const_spec((1, D_MODEL)), pl.BlockSpec(memory_space=pl.ANY)]
        args += [final_g, states]
        aliases = {len(args) - 1: 1}
    return pl.pallas_call(
        functools.partial(_ret_prompt_kernel, last=last),
        grid=(B,),
        in_specs=in_specs,
        out_specs=[x_spec,
                   pl.BlockSpec((1, 1, 2, RET_HEADS, RET_DK, RET_DV), lambda b: (b, j, 0, 0, 0, 0))],
        out_shape=[jax.ShapeDtypeStruct(x.shape, F32), jax.ShapeDtypeStruct(st_shape, F32)],
        scratch_shapes=[
            pltpu.VMEM((RET_HEADS, L, L), F32),
            pltpu.VMEM((2 * RET_HEADS, L), F32),
            pltpu.VMEM((L, D_INNER), BF16),
        ],
        input_output_aliases=aliases,
        compiler_params=pltpu.CompilerParams(
            dimension_semantics=("arbitrary",), vmem_limit_bytes=VMEM_LIMIT),
        name="ret_prompt",
    )(*args)


def _ret_sample_kernel(*refs, last):
    if last:
        (x_ref, mod_ref, ng_ref, wq_ref, wk_ref, wv_ref, wg_ref, ld_ref, gn_ref, wout_ref, s0_ref,
         cos_ref, sin_ref, fg_ref, o_ref, h_ref) = refs
    else:
        (x_ref, mod_ref, ng_ref, wq_ref, wk_ref, wv_ref, wg_ref, ld_ref, gn_ref, wout_ref, s0_ref,
         cos_ref, sin_ref, o_ref, h_ref) = refs
    L = x_ref.shape[1]
    half = RET_DK // 2
    hd = pl.program_id(1)

    @pl.when(hd == 0)
    def _():
        x = x_ref[0]
        h_ref[...] = _mod_norm(x, ng_ref[...], mod_ref).astype(BF16)
        o_ref[0] = x

    h = h_ref[...]
    cos = cos_ref[...]
    sin = sin_ref[...]

    def rope(t):
        t1, t2 = t[:, :half], t[:, half:]
        return jnp.concatenate([t1 * cos - t2 * sin, t2 * cos + t1 * sin], axis=-1)

    qh = rope(_dot(h, wq_ref[...].astype(BF16)))
    kh = rope(_dot(h, wk_ref[...].astype(BF16))).astype(BF16)
    vh = _dot(h, wv_ref[...].astype(BF16)).astype(BF16)
    gh = _dot(h, wg_ref[...].astype(BF16))
    wout = wout_ref[...].astype(BF16)
    posq = lax.broadcasted_iota(jnp.int32, (L, RET_DK), 0).astype(F32)
    lgf_q = -jnp.exp(jnp.full((1, RET_DK), ld_ref[0, hd], F32))
    lgb_q = -jnp.exp(jnp.full((1, RET_DK), ld_ref[1, hd], F32))
    ox = (_dot((qh * jnp.exp((posq + 1.0) * lgf_q)).astype(BF16), s0_ref[0, 0, 0, 0].astype(BF16))
          + _dot((qh * jnp.exp((L - posq) * lgb_q)).astype(BF16), s0_ref[0, 0, 1, 0].astype(BF16)))
    qb = qh.astype(BF16)
    gate = mod_ref[0, 2:3, :]
    n_blk = L // TILE
    ii = lax.broadcasted_iota(jnp.int32, (TILE, TILE), 0)
    jj = lax.broadcasted_iota(jnp.int32, (TILE, TILE), 1)
    diff = (ii - jj).astype(F32)
    m_diag = _decay_mask(diff, lgf_q, lgb_q)
    below = [jnp.exp((TILE + diff) * lgf_q) * RET_DK ** -0.5]
    above = [jnp.exp((TILE - diff) * lgb_q) * RET_DK ** -0.5]
    for dist in range(1, n_blk - 1):
        below.append(below[0] * jnp.exp((TILE * dist) * lgf_q))
        above.append(above[0] * jnp.exp((TILE * dist) * lgb_q))
    for r in range(n_blk):
        rows = slice(r * TILE, (r + 1) * TILE)
        m = jnp.concatenate([m_diag if cb == r else below[r - cb - 1] if cb < r else above[cb - r - 1]
                             for cb in range(n_blk)], axis=1)
        p = (_dot_nt(qb[rows], kh) * m).astype(BF16)
        o = _dot(p, vh) + ox[rows]
        z = _head_norm_gate(o, gn_ref[...], gh[rows])
        o_ref[0, rows, :] += gate * _dot(z, wout)

    if last:
        @pl.when(hd == RET_HEADS - 1)
        def _():
            o_ref[0] = _rms(o_ref[0], fg_ref[...])


def _ret_sample_layer(x, mods, ng, w_in, ld, gn, w_out, state_ret, cos, sin, final_g, *, i):
    B, L, _ = x.shape
    j = i // 2
    last = final_g is not None
    x_spec = pl.BlockSpec((1, L, D_MODEL), lambda b, hd: (b, 0, 0))
    kb = D_MODEL // RET_DK
    vb = 2 * D_MODEL // RET_DV
    in_specs = [
        x_spec,
        pl.BlockSpec((None, 1, 3, D_MODEL), lambda b, hd: (i, 1 + b, 0, 0)),
        _const_spec((1, D_MODEL), i),
        pl.BlockSpec((None, D_MODEL, RET_DK), lambda b, hd: (j, 0, hd)),
        pl.BlockSpec((None, D_MODEL, RET_DK), lambda b, hd: (j, 0, kb + hd)),
        pl.BlockSpec((None, D_MODEL, RET_DV), lambda b, hd: (j, 0, vb + hd)),
        pl.BlockSpec((None, D_MODEL, RET_DV), lambda b, hd: (j, 0, vb + RET_HEADS + hd)),
        pl.BlockSpec(memory_space=pltpu.SMEM),
        pl.BlockSpec((None, 1, RET_DV), lambda b, hd: (j, 0, hd)),
        pl.BlockSpec((None, RET_DV, D_MODEL), lambda b, hd: (j, hd, 0)),
        pl.BlockSpec((1, 1, 2, 1, RET_DK, RET_DV), lambda b, hd: (b, j, 0, hd, 0, 0)),
        _const_spec((L, RET_DK // 2)),
        _const_spec((L, RET_DK // 2)),
    ]
    args = [x, mods, ng, w_in, w_in, w_in, w_in, ld[j], gn, w_out, state_ret, cos, sin]
    if last:
        in_specs.append(_const_spec((1, D_MODEL)))
        args.append(final_g)
    return pl.pallas_call(
        functools.partial(_ret_sample_kernel, last=last),
        grid=(B, RET_HEADS),
        in_specs=in_specs,
        out_specs=x_spec,
        out_shape=jax.ShapeDtypeStruct(x.shape, F32),
        scratch_shapes=[pltpu.VMEM((L, D_MODEL), BF16)],
        compiler_params=pltpu.CompilerParams(
            dimension_semantics=("arbitrary", "arbitrary"), vmem_limit_bytes=VMEM_LIMIT),
        name="ret_sample",
    )(*args)


def _rope_tables(L):
    rows = L // GRID_W
    r = jnp.repeat(jnp.arange(rows, dtype=F32), GRID_W)
    col = jnp.tile(jnp.arange(GRID_W, dtype=F32), rows)
    nf = RET_DK // 4
    inv = ROPE_BASE ** (-jnp.arange(nf, dtype=F32) / nf)
    ang = jnp.concatenate([r[:, None] * inv, col[:, None] * inv], axis=-1)
    return jnp.cos(ang), jnp.sin(ang)


def kernel(x_prompt, x_sample, c, state_ret, c_ctx, ada_w, ada_b, norm_g, final_norm_g, conv_w_in, conv_dw, conv_dw_b, conv_ln_g, conv_ln_b, conv_w_out, ret_w_in, ret_log_decay, ret_gn_g, ret_w_out):
    n_dec = c.shape[0]
    cond8 = jnp.concatenate([c_ctx[None, :], c, jnp.zeros((8 - 1 - n_dec, D_MODEL), F32)], axis=0)
    mods = _adaln(cond8, ada_w, ada_b).reshape(DEPTH, 8, 3, D_MODEL)
    cos, sin = _rope_tables(x_sample.shape[1])
    fg = final_norm_g.reshape(1, D_MODEL)

    ng = norm_g.reshape(DEPTH, 1, D_MODEL)
    conv_args = (ng, conv_w_in.astype(BF16), conv_dw, conv_dw_b[:, None, :], conv_ln_g[:, None, :],
                 conv_ln_b[:, None, :], conv_w_out.astype(BF16))
    gn = ret_gn_g[:, None, :]

    xp, xs = x_prompt, x_sample
    states = None
    for i in range(DEPTH):
        if i % 2 == 0:
            xp = _conv_layer(xp, mods, *conv_args, i=i, mod_row=0)
            xs = _conv_layer(xs, mods, *conv_args, i=i, mod_row=1)
        else:
            final_g = fg if i == DEPTH - 1 else None
            xp, states = _ret_prompt_layer(xp, mods, ng, ret_w_in, ret_log_decay, gn, ret_w_out, final_g,
                                           states, i=i)
            xs = _ret_sample_layer(xs, mods, ng, ret_w_in, ret_log_decay, gn, ret_w_out, state_ret, cos, sin,
                                   final_g, i=i)
    return (xp, xs, states)
```

```python
import functools

import jax
import jax.numpy as jnp
from jax import lax
from jax.experimental import pallas as pl
from jax.experimental.pallas import tpu as pltpu

D_MODEL = 1024
D_INNER = 2048
DEPTH = 4
CONV_K = 31
HALO = 16
RET_HEADS = 4
RET_DK = 256
RET_DV = 512
GRID_W = 64
ROPE_BASE = 10000.0
EPS = 1e-6
TILE = 256
LANES = 128
CHUNK = 256
SEQS_PER_STEP = 2
VMEM_LIMIT = 56 * 1024 * 1024

F32 = jnp.float32
BF16 = jnp.bfloat16


def _dot(a, b):
    return jnp.dot(a, b, preferred_element_type=F32)


def _dot_nt(a, b):
    return lax.dot_general(a, b, (((1,), (1,)), ((), ())), preferred_element_type=F32)


def _sigmoid(x):
    return 1.0 / (1.0 + jnp.exp(-x))


def _silu(x):
    return x * _sigmoid(x)


def _rms(x, g):
    return x * lax.rsqrt(jnp.mean(x * x, axis=-1, keepdims=True) + EPS) * g


def _mod_norm(x, g, mod_ref):
    return _rms(x, g) * (1.0 + mod_ref[0, 1:2, :]) + mod_ref[0, 0:1, :]


def _adaln_kernel(cond_ref, w_ref, b_ref, o_ref):
    s = _silu(cond_ref[...]).astype(BF16)
    o_ref[0] = _dot(s, w_ref[0].astype(BF16)) + b_ref[0]


def _adaln(cond8, ada_w, ada_b):
    return pl.pallas_call(
        _adaln_kernel,
        grid=(DEPTH,),
        in_specs=[
            pl.BlockSpec((8, D_MODEL), lambda i: (0, 0)),
            pl.BlockSpec((1, D_MODEL, 3 * D_MODEL), lambda i: (i, 0, 0)),
            pl.BlockSpec((1, 1, 3 * D_MODEL), lambda i: (i, 0, 0)),
        ],
        out_specs=pl.BlockSpec((1, 8, 3 * D_MODEL), lambda i: (i, 0, 0)),
        out_shape=jax.ShapeDtypeStruct((DEPTH, 8, 3 * D_MODEL), F32),
        compiler_params=pltpu.CompilerParams(
            dimension_semantics=("arbitrary",), vmem_limit_bytes=VMEM_LIMIT),
        name="adaln",
    )(cond8, ada_w, ada_b.reshape(DEPTH, 1, 3 * D_MODEL))


def _exact_zero_after(v):
    bits = pltpu.bitcast(v, jnp.uint32)
    return pltpu.bitcast((bits >> 16) >> 16, F32)


def _conv_kernel(*refs, halo, tiles_per_seq):
    if halo:
        (x_ref, xt_ref, xb_ref, mod_ref, ng_ref, win_ref, dw_ref, dwb_ref, lng_ref, lnb_ref,
         wout_ref, o_ref, pad_ref, u_ref, gate_ref) = refs
    else:
        (x_ref, mod_ref, ng_ref, win_ref, dw_ref, dwb_ref, lng_ref, lnb_ref,
         wout_ref, o_ref, pad_ref, u_ref, gate_ref) = refs
    n_seq, seg = x_ref.shape[0], x_ref.shape[1]
    n_slab = CHUNK // LANES
    n_chunk = D_INNER // CHUNK
    ng = ng_ref[...]
    h = _mod_norm(x_ref[...].reshape(n_seq * seg, D_MODEL), ng, mod_ref).astype(BF16)
    if halo:
        t = pl.program_id(1)
        ht = _mod_norm(xt_ref[0], ng, mod_ref).astype(BF16)
        hb = _mod_norm(xb_ref[0], ng, mod_ref).astype(BF16)
        hext = jnp.concatenate([ht, h, hb], axis=0)
        row = lax.broadcasted_iota(jnp.int32, (seg + 2 * HALO, CHUNK), 0)
        valid = jnp.logical_and(jnp.logical_or(row >= HALO, t > 0),
                                jnp.logical_or(row < HALO + seg, t < tiles_per_seq - 1))
    else:
        hext = h
        zeros = jnp.zeros((HALO, LANES), F32)
        for s in range(D_INNER // LANES):
            for q in range(n_seq):
                pad_ref[s, q, 0:HALO, :] = zeros
                pad_ref[s, q, HALO + seg:2 * HALO + seg, :] = zeros

    def finish(f):
        rows = slice(f * TILE, (f + 1) * TILE)
        q, r0 = divmod(f * TILE, seg)
        u = u_ref[rows, :]
        mu = jnp.mean(u, axis=-1, keepdims=True)
        d = u - mu
        var = jnp.mean(d * d, axis=-1, keepdims=True)
        y = d * lax.rsqrt(var + EPS) * lng_ref[...] + lnb_ref[...]
        z = _silu(y.astype(BF16)) * gate_ref[rows, :]
        out = x_ref[q, r0:r0 + TILE, :] + mod_ref[0, 2:3, :] * _dot(z, wout_ref[...])
        o_ref[q, r0:r0 + TILE, :] = out
        return out

    n_fin = n_seq * seg // TILE
    pin = None
    for c in range(n_chunk):
        c0 = c * CHUNK
        a = _dot(hext, win_ref[:, c0:c0 + CHUNK])
        b = _dot(hext, win_ref[:, D_INNER + c0:D_INNER + c0 + CHUNK])
        g = _dot(h, win_ref[:, 2 * D_INNER + c0:2 * D_INNER + c0 + CHUNK])
        u = a * _sigmoid(b)
        gate_ref[:, c0:c0 + CHUNK] = _silu(g.astype(BF16))
        if halo:
            u = jnp.where(valid, u, 0.0)
        for l in range(n_slab):
            s = c * n_slab + l
            if halo:
                pad_ref[s, 0, :, :] = u[:, l * LANES:(l + 1) * LANES]
            else:
                for q in range(n_seq):
                    pad_ref[s, q, HALO:HALO + seg, :] = u[q * seg:(q + 1) * seg, l * LANES:(l + 1) * LANES]
        for f in range(n_fin):
            if f > 0 and c == n_chunk - 1:
                pin = _exact_zero_after(finish(f - 1)[TILE - 8:TILE, D_MODEL - LANES:D_MODEL])[0:1, :]
            for l in range(n_slab):
                s = c * n_slab + l
                lanes = slice(c0 + l * LANES, c0 + (l + 1) * LANES)
                bias = dwb_ref[0:1, lanes]
                if f > 0 and c == n_chunk - 1:
                    bias = bias + pin
                for rb in range(TILE // LANES):
                    q, r0 = divmod(f * TILE + rb * LANES, seg)
                    acc = jnp.broadcast_to(bias, (LANES, LANES))
                    for k in range(CONV_K):
                        start = r0 + k + HALO - CONV_K // 2
                        acc = acc + pad_ref[s, q, pl.ds(start, LANES), :] * dw_ref[k:k + 1, lanes]
                    u_ref[f * TILE + rb * LANES:f * TILE + (rb + 1) * LANES, lanes] = acc
    finish(n_fin - 1)


def _const_spec(shape, layer=None):
    nd = len(shape)
    if layer is None:
        return pl.BlockSpec(shape, lambda *_: (0,) * nd, pipeline_mode=pl.Buffered(1))
    return pl.BlockSpec((None,) + shape, lambda *_: (layer,) + (0,) * nd, pipeline_mode=pl.Buffered(1))


def _conv_layer(x, mods, ng, w_in, dw, dwb, lng, lnb, w_out, *, i, mod_row):
    B, L, _ = x.shape
    j = i // 2
    halo = L > TILE
    n_seq = 1 if halo else SEQS_PER_STEP
    seg = SEQS_PER_STEP * TILE // n_seq
    tiles = L // seg
    hb = seg // HALO
    x_spec = pl.BlockSpec((n_seq, seg, D_MODEL), lambda b, t: (b, t, 0))
    in_specs = [x_spec]
    args = [x]
    if halo:
        in_specs += [
            pl.BlockSpec((1, HALO, D_MODEL), lambda b, t: (b, jnp.maximum(t * hb - 1, 0), 0)),
            pl.BlockSpec((1, HALO, D_MODEL), lambda b, t: (b, jnp.minimum((t + 1) * hb, L // HALO - 1), 0)),
        ]
        args += [x, x]
    mod_map = (lambda b, t: (i, mod_row + b, 0, 0)) if mod_row else (lambda b, t: (i, 0, 0, 0))
    in_specs += [
        pl.BlockSpec((None, 1, 3, D_MODEL), mod_map),
        _const_spec((1, D_MODEL), i),
        _const_spec((D_MODEL, 3 * D_INNER), j),
        _const_spec((CONV_K, D_INNER), j),
        _const_spec((1, D_INNER), j),
        _const_spec((1, D_INNER), j),
        _const_spec((1, D_INNER), j),
        _const_spec((D_INNER, D_MODEL), j),
    ]
    args += [mods, ng, w_in, dw, dwb, lng, lnb, w_out]
    return pl.pallas_call(
        functools.partial(_conv_kernel, halo=halo, tiles_per_seq=tiles),
        grid=(B // n_seq, tiles),
        in_specs=in_specs,
        out_specs=x_spec,
        out_shape=jax.ShapeDtypeStruct(x.shape, F32),
        scratch_shapes=[
            pltpu.VMEM((D_INNER // LANES, n_seq, seg + 2 * HALO, LANES), F32),
            pltpu.VMEM((n_seq * seg, D_INNER), F32),
            pltpu.VMEM((n_seq * seg, D_INNER), BF16),
        ],
        compiler_params=pltpu.CompilerParams(
            dimension_semantics=("arbitrary", "arbitrary"), vmem_limit_bytes=VMEM_LIMIT),
        name="conv_sample" if halo else "conv_prompt",
    )(*args)


def _decay_mask(diff, lgf, lgb):
    e = jnp.exp(jnp.abs(diff) * jnp.where(diff >= 0.0, lgf, lgb))
    return e * jnp.where(diff == 0.0, 2.0 * RET_DK ** -0.5, RET_DK ** -0.5)


def _head_norm_gate(o, gn, g):
    mu = jnp.mean(o, axis=-1, keepdims=True)
    d = o - mu
    var = jnp.mean(d * d, axis=-1, keepdims=True)
    return (d * lax.rsqrt(var + EPS) * gn * _silu(g)).astype(BF16)


def _ret_prompt_kernel(*refs, last):
    if last:
        (x_ref, mod_ref, ng_ref, win_ref, ld_ref, gn_ref, wout_ref, fg_ref, _st_in,
         o_ref, st_ref, m_ref, w_ref, z_ref) = refs
    else:
        (x_ref, mod_ref, ng_ref, win_ref, ld_ref, gn_ref, wout_ref,
         o_ref, st_ref, m_ref, w_ref, z_ref) = refs
    L = TILE

    @pl.when(pl.program_id(0) == 0)
    def _():
        lg = -jnp.exp(ld_ref[...])
        ii = lax.broadcasted_iota(jnp.int32, (L, L), 0)
        jj = lax.broadcasted_iota(jnp.int32, (L, L), 1)
        diff = (ii - jj).astype(F32)
        pos = lax.broadcasted_iota(jnp.int32, (1, L), 1).astype(F32)
        for hd in range(RET_HEADS):
            lgf = lg[0:1, hd:hd + 1]
            lgb = lg[1:2, hd:hd + 1]
            m_ref[hd] = _decay_mask(diff, lgf, lgb)
            w_ref[2 * hd:2 * hd + 1, :] = jnp.exp((L - 1.0 - pos) * lgf) * RET_DK ** -0.5
            w_ref[2 * hd + 1:2 * hd + 2, :] = jnp.exp(pos * lgb) * RET_DK ** -0.5

    x = x_ref[0]
    h = _mod_norm(x, ng_ref[...], mod_ref).astype(BF16)
    q = _dot(h, win_ref[:, 0:D_MODEL].astype(BF16)).astype(BF16)
    k = _dot(h, win_ref[:, D_MODEL:2 * D_MODEL].astype(BF16))
    v = _dot(h, win_ref[:, 2 * D_MODEL:2 * D_MODEL + D_INNER].astype(BF16)).astype(BF16)
    g = _dot(h, win_ref[:, 2 * D_MODEL + D_INNER:].astype(BF16))
    kb = k.astype(BF16)
    for hd in range(RET_HEADS):
        kcols = slice(hd * RET_DK, (hd + 1) * RET_DK)
        vcols = slice(hd * RET_DV, (hd + 1) * RET_DV)
        vh = v[:, vcols]
        s = _dot_nt(q[:, kcols], kb[:, kcols])
        p = (s * m_ref[hd]).astype(BF16)
        o = _dot(p, vh)
        z_ref[:, vcols] = _head_norm_gate(o, gn_ref[0:1, vcols], g[:, vcols])
        kt = k[:, kcols].T
        st_ref[0, 0, 0, hd] = _dot((kt * w_ref[2 * hd:2 * hd + 1, :]).astype(BF16), vh)
        st_ref[0, 0, 1, hd] = _dot((kt * w_ref[2 * hd + 1:2 * hd + 2, :]).astype(BF16), vh)
    xn = x + mod_ref[0, 2:3, :] * _dot(z_ref[...], wout_ref[...].astype(BF16))
    if last:
        xn = _rms(xn, fg_ref[...])
    o_ref[0] = xn


def _ret_prompt_layer(x, mods, ng, w_in, ld, gn, w_out, final_g, states, *, i):
    B, L, _ = x.shape
    j = i // 2
    last = final_g is not None
    x_spec = pl.BlockSpec((1, L, D_MODEL), lambda b: (b, 0, 0))
    st_shape = (B, DEPTH // 2, 2, RET_HEADS, RET_DK, RET_DV)
    in_specs = [
        x_spec,
        pl.BlockSpec((None, 1, 3, D_MODEL), lambda b: (i, 0, 0, 0)),
        _const_spec((1, D_MODEL), i),
        _const_spec((D_MODEL, 2 * D_MODEL + 2 * D_INNER), j),
        _const_spec((2, RET_HEADS), j),
        _const_spec((1, D_INNER), j),
        _const_spec((D_INNER, D_MODEL), j),
    ]
    args = [x, mods, ng, w_in, ld, gn, w_out]
    aliases = {}
    if last:
        in_specs += [_const_spec((1, D_MODEL)), pl.BlockSpec(memory_space=pl.ANY)]
        args += [final_g, states]
        aliases = {len(args) - 1: 1}
    return pl.pallas_call(
        functools.partial(_ret_prompt_kernel, last=last),
        grid=(B,),
        in_specs=in_specs,
        out_specs=[x_spec,
                   pl.BlockSpec((1, 1, 2, RET_HEADS, RET_DK, RET_DV), lambda b: (b, j, 0, 0, 0, 0))],
        out_shape=[jax.ShapeDtypeStruct(x.shape, F32), jax.ShapeDtypeStruct(st_shape, F32)],
        scratch_shapes=[
            pltpu.VMEM((RET_HEADS, L, L), F32),
            pltpu.VMEM((2 * RET_HEADS, L), F32),
            pltpu.VMEM((L, D_INNER), BF16),
        ],
        input_output_aliases=aliases,
        compiler_params=pltpu.CompilerParams(
            dimension_semantics=("arbitrary",), vmem_limit_bytes=VMEM_LIMIT),
        name="ret_prompt",
    )(*args)


def _ret_sample_kernel(*refs, last):
    if last:
        (x_ref, mod_ref, ng_ref, wq_ref, wk_ref, wv_ref, wg_ref, ld_ref, gn_ref, wout_ref, s0_ref,
         cos_ref, sin_ref, fg_ref, o_ref, h_ref) = refs
    else:
        (x_ref, mod_ref, ng_ref, wq_ref, wk_ref, wv_ref, wg_ref, ld_ref, gn_ref, wout_ref, s0_ref,
         cos_ref, sin_ref, o_ref, h_ref) = refs
    L = x_ref.shape[1]
    half = RET_DK // 2
    hd = pl.program_id(1)

    @pl.when(hd == 0)
    def _():
        x = x_ref[0]
        h_ref[...] = _mod_norm(x, ng_ref[...], mod_ref).astype(BF16)
        o_ref[0] = x

    h = h_ref[...]
    cos = cos_ref[...]
    sin = sin_ref[...]

    def rope(t):
        t1, t2 = t[:, :half], t[:, half:]
        return jnp.concatenate([t1 * cos - t2 * sin, t2 * cos + t1 * sin], axis=-1)

    qh = rope(_dot(h, wq_ref[...].astype(BF16)))
    kh = rope(_dot(h, wk_ref[...].astype(BF16))).astype(BF16)
    vh = _dot(h, wv_ref[...].astype(BF16)).astype(BF16)
    gh = _dot(h, wg_ref[...].astype(BF16))
    wout = wout_ref[...].astype(BF16)
    posq = lax.broadcasted_iota(jnp.int32, (L, RET_DK), 0).astype(F32)
    lgf_q = -jnp.exp(jnp.full((1, RET_DK), ld_ref[0, hd], F32))
    lgb_q = -jnp.exp(jnp.full((1, RET_DK), ld_ref[1, hd], F32))
    ox = (_dot((qh * jnp.exp((posq + 1.0) * lgf_q)).astype(BF16), s0_ref[0, 0, 0, 0].astype(BF16))
          + _dot((qh * jnp.exp((L - posq) * lgb_q)).astype(BF16), s0_ref[0, 0, 1, 0].astype(BF16)))
    qb = qh.astype(BF16)
    gate = mod_ref[0, 2:3, :]
    n_blk = L // TILE
    ii = lax.broadcasted_iota(jnp.int32, (TILE, TILE), 0)
    jj = lax.broadcasted_iota(jnp.int32, (TILE, TILE), 1)
    diff = (ii - jj).astype(F32)
    m_diag = _decay_mask(diff, lgf_q, lgb_q)
    below = [jnp.exp((TILE + diff) * lgf_q) * RET_DK ** -0.5]
    above = [jnp.exp((TILE - diff) * lgb_q) * RET_DK ** -0.5]
    for dist in range(1, n_blk - 1):
        below.append(below[0] * jnp.exp((TILE * dist) * lgf_q))
        above.append(above[0] * jnp.exp((TILE * dist) * lgb_q))
    m = jnp.concatenate(
        [jnp.concatenate([m_diag if cb == r else below[r - cb - 1] if cb < r else above[cb - r - 1]
                          for cb in range(n_blk)], axis=1) for r in range(n_blk)], axis=0)
    p = (_dot_nt(qb, kh) * m).astype(BF16)
    o = _dot(p, vh) + ox
    z = _head_norm_gate(o, gn_ref[...], gh)
    o_ref[0] += gate * _dot(z, wout)

    if last:
        @pl.when(hd == RET_HEADS - 1)
        def _():
            o_ref[0] = _rms(o_ref[0], fg_ref[...])


def _ret_sample_layer(x, mods, ng, w_in, ld, gn, w_out, state_ret, cos, sin, final_g, *, i):
    B, L, _ = x.shape
    j = i // 2
    last = final_g is not None
    x_spec = pl.BlockSpec((1, L, D_MODEL), lambda b, hd: (b, 0, 0))
    kb = D_MODEL // RET_DK
    vb = 2 * D_MODEL // RET_DV
    in_specs = [
        x_spec,
        pl.BlockSpec((None, 1, 3, D_MODEL), lambda b, hd: (i, 1 + b, 0, 0)),
        _const_spec((1, D_MODEL), i),
        pl.BlockSpec((None, D_MODEL, RET_DK), lambda b, hd: (j, 0, hd)),
        pl.BlockSpec((None, D_MODEL, RET_DK), lambda b, hd: (j, 0, kb + hd)),
        pl.BlockSpec((None, D_MODEL, RET_DV), lambda b, hd: (j, 0, vb + hd)),
        pl.BlockSpec((None, D_MODEL, RET_DV), lambda b, hd: (j, 0, vb + RET_HEADS + hd)),
        pl.BlockSpec(memory_space=pltpu.SMEM),
        pl.BlockSpec((None, 1, RET_DV), lambda b, hd: (j, 0, hd)),
        pl.BlockSpec((None, RET_DV, D_MODEL), lambda b, hd: (j, hd, 0)),
        pl.BlockSpec((1, 1, 2, 1, RET_DK, RET_DV), lambda b, hd: (b, j, 0, hd, 0, 0)),
        _const_spec((L, RET_DK // 2)),
        _const_spec((L, RET_DK // 2)),
    ]
    args = [x, mods, ng, w_in, w_in, w_in, w_in, ld[j], gn, w_out, state_ret, cos, sin]
    if last:
        in_specs.append(_const_spec((1, D_MODEL)))
        args.append(final_g)
    return pl.pallas_call(
        functools.partial(_ret_sample_kernel, last=last),
        grid=(B, RET_HEADS),
        in_specs=in_specs,
        out_specs=x_spec,
        out_shape=jax.ShapeDtypeStruct(x.shape, F32),
        scratch_shapes=[pltpu.VMEM((L, D_MODEL), BF16)],
        compiler_params=pltpu.CompilerParams(
            dimension_semantics=("arbitrary", "arbitrary"), vmem_limit_bytes=VMEM_LIMIT),
        name="ret_sample",
    )(*args)


def _rope_tables(L):
    rows = L // GRID_W
    r = jnp.repeat(jnp.arange(rows, dtype=F32), GRID_W)
    col = jnp.tile(jnp.arange(GRID_W, dtype=F32), rows)
    nf = RET_DK // 4
    inv = ROPE_BASE ** (-jnp.arange(nf, dtype=F32) / nf)
    ang = jnp.concatenate([r[:, None] * inv, col[:, None] * inv], axis=-1)
    return jnp.cos(ang), jnp.sin(ang)


def kernel(x_prompt, x_sample, c, state_ret, c_ctx, ada_w, ada_b, norm_g, final_norm_g, conv_w_in, conv_dw, conv_dw_b, conv_ln_g, conv_ln_b, conv_w_out, ret_w_in, ret_log_decay, ret_gn_g, ret_w_out):
    n_dec = c.shape[0]
    cond8 = jnp.concatenate([c_ctx[None, :], c, jnp.zeros((8 - 1 - n_dec, D_MODEL), F32)], axis=0)
    mods = _adaln(cond8, ada_w, ada_b).reshape(DEPTH, 8, 3, D_MODEL)
    cos, sin = _rope_tables(x_sample.shape[1])
    fg = final_norm_g.reshape(1, D_MODEL)

    ng = norm_g.reshape(DEPTH, 1, D_MODEL)
    conv_args = (ng, conv_w_in.astype(BF16), conv_dw, conv_dw_b[:, None, :], conv_ln_g[:, None, :],
                 conv_ln_b[:, None, :], conv_w_out.astype(BF16))
    gn = ret_gn_g[:, None, :]

    xp, xs = x_prompt, x_sample
    states = None
    for i in range(DEPTH):
        if i % 2 == 0:
            xp = _conv_layer(xp, mods, *conv_args, i=i, mod_row=0)
            xs = _conv_layer(xs, mods, *conv_args, i=i, mod_row=1)
        else:
            final_g = fg if i == DEPTH - 1 else None
            xp, states = _ret_prompt_layer(xp, mods, ng, ret_w_in, ret_log_decay, gn, ret_w_out, final_g,
                                           states, i=i)
            xs = _ret_sample_layer(xs, mods, ng, ret_w_in, ret_log_decay, gn, ret_w_out, state_ret, cos, sin,
                                   final_g, i=i)
    return (xp, xs, states)
```

```python
import functools

import jax
import jax.numpy as jnp
from jax import lax
from jax.experimental import pallas as pl
from jax.experimental.pallas import tpu as pltpu

D_MODEL = 1024
D_INNER = 2048
DEPTH = 4
CONV_K = 31
HALO = 16
RET_HEADS = 4
RET_DK = 256
RET_DV = 512
GRID_W = 64
ROPE_BASE = 10000.0
EPS = 1e-6
TILE = 256
LANES = 128
CHUNK = 256
SEQS_PER_STEP = 2
VMEM_LIMIT = 56 * 1024 * 1024

F32 = jnp.float32
BF16 = jnp.bfloat16


def _dot(a, b):
    return jnp.dot(a, b, preferred_element_type=F32)


def _dot_nt(a, b):
    return lax.dot_general(a, b, (((1,), (1,)), ((), ())), preferred_element_type=F32)


def _sigmoid(x):
    return 1.0 / (1.0 + jnp.exp(-x))


def _silu(x):
    return x * _sigmoid(x)


def _rms(x, g):
    return x * lax.rsqrt(jnp.mean(x * x, axis=-1, keepdims=True) + EPS) * g


def _mod_norm(x, g, mod_ref):
    return _rms(x, g) * (1.0 + mod_ref[0, 1:2, :]) + mod_ref[0, 0:1, :]


def _adaln_kernel(cond_ref, w_ref, b_ref, o_ref):
    s = _silu(cond_ref[...]).astype(BF16)
    o_ref[0] = _dot(s, w_ref[0].astype(BF16)) + b_ref[0]


def _adaln(cond8, ada_w, ada_b):
    return pl.pallas_call(
        _adaln_kernel,
        grid=(DEPTH,),
        in_specs=[
            pl.BlockSpec((8, D_MODEL), lambda i: (0, 0)),
            pl.BlockSpec((1, D_MODEL, 3 * D_MODEL), lambda i: (i, 0, 0)),
            pl.BlockSpec((1, 1, 3 * D_MODEL), lambda i: (i, 0, 0)),
        ],
        out_specs=pl.BlockSpec((1, 8, 3 * D_MODEL), lambda i: (i, 0, 0)),
        out_shape=jax.ShapeDtypeStruct((DEPTH, 8, 3 * D_MODEL), F32),
        compiler_params=pltpu.CompilerParams(
            dimension_semantics=("arbitrary",), vmem_limit_bytes=VMEM_LIMIT),
        name="adaln",
    )(cond8, ada_w, ada_b.reshape(DEPTH, 1, 3 * D_MODEL))


def _cast_kernel(a_ref, b_ref, ao_ref, bo_ref):
    ao_ref[...] = a_ref[...].astype(BF16)
    bo_ref[...] = b_ref[...].astype(BF16)


def _cast_conv_weights(w_in, w_out, j, steps=8):
    specs_in, specs_out, shapes = [], [], []
    for w in (w_in, w_out):
        rows, cols = w.shape[1] // steps, w.shape[2]
        specs_in.append(pl.BlockSpec((None, rows, cols), lambda s: (j, s, 0)))
        specs_out.append(pl.BlockSpec((rows, cols), lambda s: (s, 0)))
        shapes.append(jax.ShapeDtypeStruct(w.shape[1:], BF16))
    return pl.pallas_call(
        _cast_kernel, grid=(steps,), in_specs=specs_in, out_specs=specs_out, out_shape=shapes,
        compiler_params=pltpu.CompilerParams(dimension_semantics=("arbitrary",), vmem_limit_bytes=VMEM_LIMIT),
        name="cast_conv_weights",
    )(w_in, w_out)


def _exact_zero_after(v):
    bits = pltpu.bitcast(v, jnp.uint32)
    return pltpu.bitcast((bits >> 16) >> 16, F32)


def _conv_kernel(*refs, halo, tiles_per_seq):
    if halo:
        (x_ref, xt_ref, xb_ref, mod_ref, ng_ref, win_ref, dw_ref, dwb_ref, lng_ref, lnb_ref,
         wout_ref, o_ref, pad_ref, u_ref, gate_ref) = refs
    else:
        (x_ref, mod_ref, ng_ref, win_ref, dw_ref, dwb_ref, lng_ref, lnb_ref,
         wout_ref, o_ref, pad_ref, u_ref, gate_ref) = refs
    n_seq, seg = x_ref.shape[0], x_ref.shape[1]
    n_slab = CHUNK // LANES
    n_chunk = D_INNER // CHUNK
    ng = ng_ref[...]
    h = _mod_norm(x_ref[...].reshape(n_seq * seg, D_MODEL), ng, mod_ref).astype(BF16)
    if halo:
        t = pl.program_id(1)
        ht = _mod_norm(xt_ref[0], ng, mod_ref).astype(BF16)
        hb = _mod_norm(xb_ref[0], ng, mod_ref).astype(BF16)
        hext = jnp.concatenate([ht, h, hb], axis=0)
        row = lax.broadcasted_iota(jnp.int32, (seg + 2 * HALO, CHUNK), 0)
        valid = jnp.logical_and(jnp.logical_or(row >= HALO, t > 0),
                                jnp.logical_or(row < HALO + seg, t < tiles_per_seq - 1))
    else:
        hext = h
        zeros = jnp.zeros((HALO, LANES), F32)
        for s in range(D_INNER // LANES):
            for q in range(n_seq):
                pad_ref[s, q, 0:HALO, :] = zeros
                pad_ref[s, q, HALO + seg:2 * HALO + seg, :] = zeros

    def finish(f):
        rows = slice(f * TILE, (f + 1) * TILE)
        q, r0 = divmod(f * TILE, seg)
        u = u_ref[rows, :]
        mu = jnp.mean(u, axis=-1, keepdims=True)
        d = u - mu
        var = jnp.mean(d * d, axis=-1, keepdims=True)
        y = d * lax.rsqrt(var + EPS) * lng_ref[...] + lnb_ref[...]
        z = _silu(y.astype(BF16)) * gate_ref[rows, :]
        out = x_ref[q, r0:r0 + TILE, :] + mod_ref[0, 2:3, :] * _dot(z, wout_ref[...])
        o_ref[q, r0:r0 + TILE, :] = out
        return out

    n_fin = n_seq * seg // TILE
    pin = None
    for c in range(n_chunk):
        c0 = c * CHUNK
        a = _dot(hext, win_ref[:, c0:c0 + CHUNK])
        b = _dot(hext, win_ref[:, D_INNER + c0:D_INNER + c0 + CHUNK])
        g = _dot(h, win_ref[:, 2 * D_INNER + c0:2 * D_INNER + c0 + CHUNK])
        u = a * _sigmoid(b)
        gate_ref[:, c0:c0 + CHUNK] = _silu(g.astype(BF16))
        if halo:
            u = jnp.where(valid, u, 0.0)
        for l in range(n_slab):
            s = c * n_slab + l
            if halo:
                pad_ref[s, 0, :, :] = u[:, l * LANES:(l + 1) * LANES]
            else:
                for q in range(n_seq):
                    pad_ref[s, q, HALO:HALO + seg, :] = u[q * seg:(q + 1) * seg, l * LANES:(l + 1) * LANES]
        for f in range(n_fin):
            if f > 0 and c == n_chunk - 1:
                pin = _exact_zero_after(finish(f - 1)[TILE - 8:TILE, D_MODEL - LANES:D_MODEL])[0:1, :]
            for l in range(n_slab):
                s = c * n_slab + l
                lanes = slice(c0 + l * LANES, c0 + (l + 1) * LANES)
                bias = dwb_ref[0:1, lanes]
                if f > 0 and c == n_chunk - 1:
                    bias = bias + pin
                for rb in range(TILE // LANES):
                    q, r0 = divmod(f * TILE + rb * LANES, seg)
                    acc = jnp.broadcast_to(bias, (LANES, LANES))
                    for k in range(CONV_K):
                        start = r0 + k + HALO - CONV_K // 2
                        acc = acc + pad_ref[s, q, pl.ds(start, LANES), :] * dw_ref[k:k + 1, lanes]
                    u_ref[f * TILE + rb * LANES:f * TILE + (rb + 1) * LANES, lanes] = acc
    finish(n_fin - 1)


def _const_spec(shape, layer=None):
    nd = len(shape)
    if layer is None:
        return pl.BlockSpec(shape, lambda *_: (0,) * nd, pipeline_mode=pl.Buffered(1))
    return pl.BlockSpec((None,) + shape, lambda *_: (layer,) + (0,) * nd, pipeline_mode=pl.Buffered(1))


def _conv_layer(x, mods, ng, w_in, dw, dwb, lng, lnb, w_out, *, i, mod_row):
    B, L, _ = x.shape
    j = i // 2
    halo = L > TILE
    n_seq = 1 if halo else SEQS_PER_STEP
    seg = SEQS_PER_STEP * TILE // n_seq
    tiles = L // seg
    hb = seg // HALO
    x_spec = pl.BlockSpec((n_seq, seg, D_MODEL), lambda b, t: (b, t, 0))
    in_specs = [x_spec]
    args = [x]
    if halo:
        in_specs += [
            pl.BlockSpec((1, HALO, D_MODEL), lambda b, t: (b, jnp.maximum(t * hb - 1, 0), 0)),
            pl.BlockSpec((1, HALO, D_MODEL), lambda b, t: (b, jnp.minimum((t + 1) * hb, L // HALO - 1), 0)),
        ]
        args += [x, x]
    mod_map = (lambda b, t: (i, mod_row + b, 0, 0)) if mod_row else (lambda b, t: (i, 0, 0, 0))
    in_specs += [
        pl.BlockSpec((None, 1, 3, D_MODEL), mod_map),
        _const_spec((1, D_MODEL), i),
        _const_spec((D_MODEL, 3 * D_INNER)),
        _const_spec((CONV_K, D_INNER), j),
        _const_spec((1, D_INNER), j),
        _const_spec((1, D_INNER), j),
        _const_spec((1, D_INNER), j),
        _const_spec((D_INNER, D_MODEL)),
    ]
    args += [mods, ng, w_in, dw, dwb, lng, lnb, w_out]
    return pl.pallas_call(
        functools.partial(_conv_kernel, halo=halo, tiles_per_seq=tiles),
        grid=(B // n_seq, tiles),
        in_specs=in_specs,
        out_specs=x_spec,
        out_shape=jax.ShapeDtypeStruct(x.shape, F32),
        scratch_shapes=[
            pltpu.VMEM((D_INNER // LANES, n_seq, seg + 2 * HALO, LANES), F32),
            pltpu.VMEM((n_seq * seg, D_INNER), F32),
            pltpu.VMEM((n_seq * seg, D_INNER), BF16),
        ],
        compiler_params=pltpu.CompilerParams(
            dimension_semantics=("arbitrary", "arbitrary"), vmem_limit_bytes=VMEM_LIMIT),
        name="conv_sample" if halo else "conv_prompt",
    )(*args)


def _decay_mask(diff, lgf, lgb):
    e = jnp.exp(jnp.abs(diff) * jnp.where(diff >= 0.0, lgf, lgb))
    return e * jnp.where(diff == 0.0, 2.0 * RET_DK ** -0.5, RET_DK ** -0.5)


def _head_norm_gate(o, gn, g):
    mu = jnp.mean(o, axis=-1, keepdims=True)
    d = o - mu
    var = jnp.mean(d * d, axis=-1, keepdims=True)
    return (d * lax.rsqrt(var + EPS) * gn * _silu(g)).astype(BF16)


def _ret_prompt_kernel(*refs, last):
    if last:
        (x_ref, mod_ref, ng_ref, win_ref, ld_ref, gn_ref, wout_ref, fg_ref, _st_in,
         o_ref, st_ref, m_ref, w_ref, z_ref) = refs
    else:
        (x_ref, mod_ref, ng_ref, win_ref, ld_ref, gn_ref, wout_ref, cwin_ref, cwout_ref,
         o_ref, st_ref, cwin_o_ref, cwout_o_ref, m_ref, w_ref, z_ref) = refs
        cwin_o_ref[...] = cwin_ref[...].astype(BF16)
        cwout_o_ref[...] = cwout_ref[...].astype(BF16)
    L = TILE

    @pl.when(pl.program_id(0) == 0)
    def _():
        lg = -jnp.exp(ld_ref[...])
        ii = lax.broadcasted_iota(jnp.int32, (L, L), 0)
        jj = lax.broadcasted_iota(jnp.int32, (L, L), 1)
        diff = (ii - jj).astype(F32)
        pos = lax.broadcasted_iota(jnp.int32, (1, L), 1).astype(F32)
        for hd in range(RET_HEADS):
            lgf = lg[0:1, hd:hd + 1]
            lgb = lg[1:2, hd:hd + 1]
            m_ref[hd] = _decay_mask(diff, lgf, lgb)
            w_ref[2 * hd:2 * hd + 1, :] = jnp.exp((L - 1.0 - pos) * lgf) * RET_DK ** -0.5
            w_ref[2 * hd + 1:2 * hd + 2, :] = jnp.exp(pos * lgb) * RET_DK ** -0.5

    x = x_ref[0]
    h = _mod_norm(x, ng_ref[...], mod_ref).astype(BF16)
    q = _dot(h, win_ref[:, 0:D_MODEL].astype(BF16)).astype(BF16)
    k = _dot(h, win_ref[:, D_MODEL:2 * D_MODEL].astype(BF16))
    v = _dot(h, win_ref[:, 2 * D_MODEL:2 * D_MODEL + D_INNER].astype(BF16)).astype(BF16)
    g = _dot(h, win_ref[:, 2 * D_MODEL + D_INNER:].astype(BF16))
    kb = k.astype(BF16)
    for hd in range(RET_HEADS):
        kcols = slice(hd * RET_DK, (hd + 1) * RET_DK)
        vcols = slice(hd * RET_DV, (hd + 1) * RET_DV)
        vh = v[:, vcols]
        s = _dot_nt(q[:, kcols], kb[:, kcols])
        p = (s * m_ref[hd]).astype(BF16)
        o = _dot(p, vh)
        z_ref[:, vcols] = _head_norm_gate(o, gn_ref[0:1, vcols], g[:, vcols])
        kt = k[:, kcols].T
        st_ref[0, 0, 0, hd] = _dot((kt * w_ref[2 * hd:2 * hd + 1, :]).astype(BF16), vh)
        st_ref[0, 0, 1, hd] = _dot((kt * w_ref[2 * hd + 1:2 * hd + 2, :]).astype(BF16), vh)
    xn = x + mod_ref[0, 2:3, :] * _dot(z_ref[...], wout_ref[...].astype(BF16))
    if last:
        xn = _rms(xn, fg_ref[...])
    o_ref[0] = xn


def _ret_prompt_layer(x, mods, ng, w_in, ld, gn, w_out, final_g, states, next_conv_w, *, i):
    B, L, _ = x.shape
    j = i // 2
    last = final_g is not None
    x_spec = pl.BlockSpec((1, L, D_MODEL), lambda b: (b, 0, 0))
    st_shape = (B, DEPTH // 2, 2, RET_HEADS, RET_DK, RET_DV)
    in_specs = [
        x_spec,
        pl.BlockSpec((None, 1, 3, D_MODEL), lambda b: (i, 0, 0, 0)),
        _const_spec((1, D_MODEL), i),
        _const_spec((D_MODEL, 2 * D_MODEL + 2 * D_INNER), j),
        _const_spec((2, RET_HEADS), j),
        _const_spec((1, D_INNER), j),
        _const_spec((D_INNER, D_MODEL), j),
    ]
    args = [x, mods, ng, w_in, ld, gn, w_out]
    aliases = {}
    if last:
        in_specs += [_const_spec((1, D_MODEL)), pl.BlockSpec(memory_space=pl.ANY)]
        args += [final_g, states]
        aliases = {len(args) - 1: 1}
    out_specs = [x_spec, pl.BlockSpec((1, 1, 2, RET_HEADS, RET_DK, RET_DV), lambda b: (b, j, 0, 0, 0, 0))]
    out_shape = [jax.ShapeDtypeStruct(x.shape, F32), jax.ShapeDtypeStruct(st_shape, F32)]
    if not last:
        for w in next_conv_w:
            rows, cols = w.shape[1] // B, w.shape[2]
            in_specs.append(pl.BlockSpec((None, rows, cols), lambda b: (j + 1, b, 0)))
            args.append(w)
            out_specs.append(pl.BlockSpec((rows, cols), lambda b: (b, 0)))
            out_shape.append(jax.ShapeDtypeStruct(w.shape[1:], BF16))
    return pl.pallas_call(
        functools.partial(_ret_prompt_kernel, last=last),
        grid=(B,),
        in_specs=in_specs,
        out_specs=out_specs,
        out_shape=out_shape,
        scratch_shapes=[
            pltpu.VMEM((RET_HEADS, L, L), F32),
            pltpu.VMEM((2 * RET_HEADS, L), F32),
            pltpu.VMEM((L, D_INNER), BF16),
        ],
        input_output_aliases=aliases,
        compiler_params=pltpu.CompilerParams(
            dimension_semantics=("arbitrary",), vmem_limit_bytes=VMEM_LIMIT),
        name="ret_prompt",
    )(*args)


def _ret_sample_kernel(*refs, last):
    if last:
        (x_ref, mod_ref, ng_ref, wq_ref, wk_ref, wv_ref, wg_ref, ld_ref, gn_ref, wout_ref, s0_ref,
         cos_ref, sin_ref, fg_ref, o_ref, h_ref) = refs
    else:
        (x_ref, mod_ref, ng_ref, wq_ref, wk_ref, wv_ref, wg_ref, ld_ref, gn_ref, wout_ref, s0_ref,
         cos_ref, sin_ref, o_ref, h_ref) = refs
    L = x_ref.shape[1]
    half = RET_DK // 2
    hd = pl.program_id(1)

    @pl.when(hd == 0)
    def _():
        x = x_ref[0]
        h_ref[...] = _mod_norm(x, ng_ref[...], mod_ref).astype(BF16)
        o_ref[0] = x

    h = h_ref[...]
    cos = cos_ref[...]
    sin = sin_ref[...]

    def rope(t):
        t1, t2 = t[:, :half], t[:, half:]
        return jnp.concatenate([t1 * cos - t2 * sin, t2 * cos + t1 * sin], axis=-1)

    qh = rope(_dot(h, wq_ref[...].astype(BF16)))
    kh = rope(_dot(h, wk_ref[...].astype(BF16))).astype(BF16)
    vh = _dot(h, wv_ref[...].astype(BF16)).astype(BF16)
    gh = _dot(h, wg_ref[...].astype(BF16))
    wout = wout_ref[...].astype(BF16)
    posq = lax.broadcasted_iota(jnp.int32, (L, RET_DK), 0).astype(F32)
    lgf_q = -jnp.exp(jnp.full((1, RET_DK), ld_ref[0, hd], F32))
    lgb_q = -jnp.exp(jnp.full((1, RET_DK), ld_ref[1, hd], F32))
    ox = (_dot((qh * jnp.exp((posq + 1.0) * lgf_q)).astype(BF16), s0_ref[0, 0, 0, 0].astype(BF16))
          + _dot((qh * jnp.exp((L - posq) * lgb_q)).astype(BF16), s0_ref[0, 0, 1, 0].astype(BF16)))
    qb = qh.astype(BF16)
    gate = mod_ref[0, 2:3, :]
    n_blk = L // TILE
    ii = lax.broadcasted_iota(jnp.int32, (TILE, TILE), 0)
    jj = lax.broadcasted_iota(jnp.int32, (TILE, TILE), 1)
    diff = (ii - jj).astype(F32)
    m_diag = _decay_mask(diff, lgf_q, lgb_q)
    below = [jnp.exp((TILE + diff) * lgf_q) * RET_DK ** -0.5]
    above = [jnp.exp((TILE - diff) * lgb_q) * RET_DK ** -0.5]
    for dist in range(1, n_blk - 1):
        below.append(below[0] * jnp.exp((TILE * dist) * lgf_q))
        above.append(above[0] * jnp.exp((TILE * dist) * lgb_q))
    m = jnp.concatenate(
        [jnp.concatenate([m_diag if cb == r else below[r - cb - 1] if cb < r else above[cb - r - 1]
                          for cb in range(n_blk)], axis=1) for r in range(n_blk)], axis=0)
    p = (_dot_nt(qb, kh) * m).astype(BF16)
    o = _dot(p, vh) + ox
    z = _head_norm_gate(o, gn_ref[...], gh)
    o_ref[0] += gate * _dot(z, wout)

    if last:
        @pl.when(hd == RET_HEADS - 1)
        def _():
            o_ref[0] = _rms(o_ref[0], fg_ref[...])


def _ret_sample_layer(x, mods, ng, w_in, ld, gn, w_out, state_ret, cos, sin, final_g, *, i):
    B, L, _ = x.shape
    j = i // 2
    last = final_g is not None
    x_spec = pl.BlockSpec((1, L, D_MODEL), lambda b, hd: (b, 0, 0))
    kb = D_MODEL // RET_DK
    vb = 2 * D_MODEL // RET_DV
    in_specs = [
        x_spec,
        pl.BlockSpec((None, 1, 3, D_MODEL), lambda b, hd: (i, 1 + b, 0, 0)),
        _const_spec((1, D_MODEL), i),
        pl.BlockSpec((None, D_MODEL, RET_DK), lambda b, hd: (j, 0, hd)),
        pl.BlockSpec((None, D_MODEL, RET_DK), lambda b, hd: (j, 0, kb + hd)),
        pl.BlockSpec((None, D_MODEL, RET_DV), lambda b, hd: (j, 0, vb + hd)),
        pl.BlockSpec((None, D_MODEL, RET_DV), lambda b, hd: (j, 0, vb + RET_HEADS + hd)),
        pl.BlockSpec(memory_space=pltpu.SMEM),
        pl.BlockSpec((None, 1, RET_DV), lambda b, hd: (j, 0, hd)),
        pl.BlockSpec((None, RET_DV, D_MODEL), lambda b, hd: (j, hd, 0)),
        pl.BlockSpec((1, 1, 2, 1, RET_DK, RET_DV), lambda b, hd: (b, j, 0, hd, 0, 0)),
        _const_spec((L, RET_DK // 2)),
        _const_spec((L, RET_DK // 2)),
    ]
    args = [x, mods, ng, w_in, w_in, w_in, w_in, ld[j], gn, w_out, state_ret, cos, sin]
    if last:
        in_specs.append(_const_spec((1, D_MODEL)))
        args.append(final_g)
    return pl.pallas_call(
        functools.partial(_ret_sample_kernel, last=last),
        grid=(B, RET_HEADS),
        in_specs=in_specs,
        out_specs=x_spec,
        out_shape=jax.ShapeDtypeStruct(x.shape, F32),
        scratch_shapes=[pltpu.VMEM((L, D_MODEL), BF16)],
        compiler_params=pltpu.CompilerParams(
            dimension_semantics=("arbitrary", "arbitrary"), vmem_limit_bytes=VMEM_LIMIT),
        name="ret_sample",
    )(*args)


def _rope_tables(L):
    rows = L // GRID_W
    r = jnp.repeat(jnp.arange(rows, dtype=F32), GRID_W)
    col = jnp.tile(jnp.arange(GRID_W, dtype=F32), rows)
    nf = RET_DK // 4
    inv = ROPE_BASE ** (-jnp.arange(nf, dtype=F32) / nf)
    ang = jnp.concatenate([r[:, None] * inv, col[:, None] * inv], axis=-1)
    return jnp.cos(ang), jnp.sin(ang)


def kernel(x_prompt, x_sample, c, state_ret, c_ctx, ada_w, ada_b, norm_g, final_norm_g, conv_w_in, conv_dw, conv_dw_b, conv_ln_g, conv_ln_b, conv_w_out, ret_w_in, ret_log_decay, ret_gn_g, ret_w_out):
    n_dec = c.shape[0]
    cond8 = jnp.concatenate([c_ctx[None, :], c, jnp.zeros((8 - 1 - n_dec, D_MODEL), F32)], axis=0)
    mods = _adaln(cond8, ada_w, ada_b).reshape(DEPTH, 8, 3, D_MODEL)
    cos, sin = _rope_tables(x_sample.shape[1])
    fg = final_norm_g.reshape(1, D_MODEL)

    ng = norm_g.reshape(DEPTH, 1, D_MODEL)
    conv_rest = (conv_dw, conv_dw_b[:, None, :], conv_ln_g[:, None, :], conv_ln_b[:, None, :])
    gn = ret_gn_g[:, None, :]

    xp, xs = x_prompt, x_sample
    states = None
    cw_in, cw_out = _cast_conv_weights(conv_w_in, conv_w_out, 0)
    for i in range(DEPTH):
        if i % 2 == 0:
            xp = _conv_layer(xp, mods, ng, cw_in, *conv_rest, cw_out, i=i, mod_row=0)
            xs = _conv_layer(xs, mods, ng, cw_in, *conv_rest, cw_out, i=i, mod_row=1)
        else:
            final_g = fg if i == DEPTH - 1 else None
            res = _ret_prompt_layer(xp, mods, ng, ret_w_in, ret_log_decay, gn, ret_w_out, final_g, states,
                                    (conv_w_in, conv_w_out), i=i)
            xp, states = res[0], res[1]
            if final_g is None:
                cw_in, cw_out = res[2], res[3]
            xs = _ret_sample_layer(xs, mods, ng, ret_w_in, ret_log_decay, gn, ret_w_out, state_ret, cos, sin,
                                   final_g, i=i)
    return (xp, xs, states)
```

```python
import functools

import jax
import jax.numpy as jnp
from jax import lax
from jax.experimental import pallas as pl
from jax.experimental.pallas import tpu as pltpu

D_MODEL = 1024
D_INNER = 2048
DEPTH = 4
CONV_K = 31
HALO = 16
RET_HEADS = 4
RET_DK = 256
RET_DV = 512
GRID_W = 64
ROPE_BASE = 10000.0
EPS = 1e-6
TILE = 256
LANES = 128
CHUNK = 256
SEQS_PER_STEP = 2
VMEM_LIMIT = 56 * 1024 * 1024

F32 = jnp.float32
BF16 = jnp.bfloat16


def _dot(a, b):
    return jnp.dot(a, b, preferred_element_type=F32)


def _dot_nt(a, b):
    return lax.dot_general(a, b, (((1,), (1,)), ((), ())), preferred_element_type=F32)


def _sigmoid(x):
    return 1.0 / (1.0 + jnp.exp(-x))


def _silu(x):
    return x * _sigmoid(x)


def _rms(x, g):
    return x * lax.rsqrt(jnp.mean(x * x, axis=-1, keepdims=True) + EPS) * g


def _mod_norm(x, g, mod_ref):
    r = lax.rsqrt(jnp.mean(x * x, axis=-1, keepdims=True) + EPS)
    return x * r * (g * (1.0 + mod_ref[0, 1:2, :])) + mod_ref[0, 0:1, :]


def _adaln_kernel(cond_ref, w_ref, b_ref, o_ref):
    s = _silu(cond_ref[...]).astype(BF16)
    o_ref[0] = _dot(s, w_ref[0].astype(BF16)) + b_ref[0]


def _adaln(cond8, ada_w, ada_b):
    return pl.pallas_call(
        _adaln_kernel,
        grid=(DEPTH,),
        in_specs=[
            pl.BlockSpec((8, D_MODEL), lambda i: (0, 0)),
            pl.BlockSpec((1, D_MODEL, 3 * D_MODEL), lambda i: (i, 0, 0)),
            pl.BlockSpec((1, 1, 3 * D_MODEL), lambda i: (i, 0, 0)),
        ],
        out_specs=pl.BlockSpec((1, 8, 3 * D_MODEL), lambda i: (i, 0, 0)),
        out_shape=jax.ShapeDtypeStruct((DEPTH, 8, 3 * D_MODEL), F32),
        compiler_params=pltpu.CompilerParams(
            dimension_semantics=("arbitrary",), vmem_limit_bytes=VMEM_LIMIT),
        name="adaln",
    )(cond8, ada_w, ada_b.reshape(DEPTH, 1, 3 * D_MODEL))


def _cast_job(ws, layer, steps, step_of):
    in_specs, out_specs, out_shape = [], [], []
    for w in ws:
        rows, cols = w.shape[1] // steps, w.shape[2]
        in_specs.append(pl.BlockSpec((None, rows, cols), lambda *g: (layer, step_of(*g), 0)))
        out_specs.append(pl.BlockSpec((rows, cols), lambda *g: (step_of(*g), 0)))
        out_shape.append(jax.ShapeDtypeStruct(w.shape[1:], BF16))
    return in_specs, out_specs, out_shape


def _cast_kernel(a_ref, b_ref, ao_ref, bo_ref):
    ao_ref[...] = a_ref[...].astype(BF16)
    bo_ref[...] = b_ref[...].astype(BF16)


def _cast_conv_weights(w_in, w_out, j, steps=8):
    specs_in, specs_out, shapes = _cast_job((w_in, w_out), j, steps, lambda s: s)
    return pl.pallas_call(
        _cast_kernel, grid=(steps,), in_specs=specs_in, out_specs=specs_out, out_shape=shapes,
        compiler_params=pltpu.CompilerParams(dimension_semantics=("arbitrary",), vmem_limit_bytes=VMEM_LIMIT),
        name="cast_conv_weights",
    )(w_in, w_out)


def _exact_zero_after(v):
    bits = pltpu.bitcast(v, jnp.uint32)
    return pltpu.bitcast((bits >> 16) >> 16, F32)


def _conv_kernel(*refs, halo, tiles_per_seq):
    if halo:
        (x_ref, xt_ref, xb_ref, mod_ref, ng_ref, win_ref, dw_ref, dwb_ref, lng_ref, lnb_ref,
         wout_ref, o_ref, pad_ref, u_ref, gate_ref) = refs
    else:
        (x_ref, mod_ref, ng_ref, win_ref, dw_ref, dwb_ref, lng_ref, lnb_ref, wout_ref, rwin_ref, rwout_ref,
         o_ref, rwin_o_ref, rwout_o_ref, pad_ref, u_ref, gate_ref) = refs
        rwin_o_ref[...] = rwin_ref[...].astype(BF16)
        rwout_o_ref[...] = rwout_ref[...].astype(BF16)
    n_seq, seg = x_ref.shape[0], x_ref.shape[1]
    n_slab = CHUNK // LANES
    n_chunk = D_INNER // CHUNK
    ng = ng_ref[...]
    h = _mod_norm(x_ref[...].reshape(n_seq * seg, D_MODEL), ng, mod_ref).astype(BF16)
    if halo:
        t = pl.program_id(1)
        ht = _mod_norm(xt_ref[0], ng, mod_ref).astype(BF16)
        hb = _mod_norm(xb_ref[0], ng, mod_ref).astype(BF16)
        hext = jnp.concatenate([ht, h, hb], axis=0)
        row = lax.broadcasted_iota(jnp.int32, (seg + 2 * HALO, CHUNK), 0)
        valid = jnp.logical_and(jnp.logical_or(row >= HALO, t > 0),
                                jnp.logical_or(row < HALO + seg, t < tiles_per_seq - 1))
    else:
        hext = h
        zeros = jnp.zeros((HALO, LANES), F32)
        for s in range(D_INNER // LANES):
            for q in range(n_seq):
                pad_ref[s, q, 0:HALO, :] = zeros
                pad_ref[s, q, HALO + seg:2 * HALO + seg, :] = zeros

    def finish(f):
        rows = slice(f * TILE, (f + 1) * TILE)
        q, r0 = divmod(f * TILE, seg)
        u = u_ref[rows, :]
        mu = jnp.mean(u, axis=-1, keepdims=True)
        d = u - mu
        var = jnp.mean(d * d, axis=-1, keepdims=True)
        y = d * lax.rsqrt(var + EPS) * lng_ref[...] + lnb_ref[...]
        z = _silu(y.astype(BF16)) * gate_ref[rows, :]
        out = x_ref[q, r0:r0 + TILE, :] + mod_ref[0, 2:3, :] * _dot(z, wout_ref[...])
        o_ref[q, r0:r0 + TILE, :] = out
        return out

    n_fin = n_seq * seg // TILE
    pin = None
    for c in range(n_chunk):
        c0 = c * CHUNK
        a = _dot(hext, win_ref[:, c0:c0 + CHUNK])
        b = _dot(hext, win_ref[:, D_INNER + c0:D_INNER + c0 + CHUNK])
        g = _dot(h, win_ref[:, 2 * D_INNER + c0:2 * D_INNER + c0 + CHUNK])
        u = a * _sigmoid(b)
        gate_ref[:, c0:c0 + CHUNK] = _silu(g.astype(BF16))
        if halo:
            u = jnp.where(valid, u, 0.0)
        for l in range(n_slab):
            s = c * n_slab + l
            if halo:
                pad_ref[s, 0, :, :] = u[:, l * LANES:(l + 1) * LANES]
            else:
                for q in range(n_seq):
                    pad_ref[s, q, HALO:HALO + seg, :] = u[q * seg:(q + 1) * seg, l * LANES:(l + 1) * LANES]
        for f in range(n_fin):
            if f > 0 and c == n_chunk - 1:
                pin = _exact_zero_after(finish(f - 1)[TILE - 8:TILE, D_MODEL - LANES:D_MODEL])[0:1, :]
            for l in range(n_slab):
                s = c * n_slab + l
                lanes = slice(c0 + l * LANES, c0 + (l + 1) * LANES)
                bias = dwb_ref[0:1, lanes]
                if f > 0 and c == n_chunk - 1:
                    bias = bias + pin
                for rb in range(TILE // LANES):
                    q, r0 = divmod(f * TILE + rb * LANES, seg)
                    acc = jnp.broadcast_to(bias, (LANES, LANES))
                    for k in range(CONV_K):
                        start = r0 + k + HALO - CONV_K // 2
                        acc = acc + pad_ref[s, q, pl.ds(start, LANES), :] * dw_ref[k:k + 1, lanes]
                    u_ref[f * TILE + rb * LANES:f * TILE + (rb + 1) * LANES, lanes] = acc
    finish(n_fin - 1)


def _const_spec(shape, layer=None):
    nd = len(shape)
    if layer is None:
        return pl.BlockSpec(shape, lambda *_: (0,) * nd, pipeline_mode=pl.Buffered(1))
    return pl.BlockSpec((None,) + shape, lambda *_: (layer,) + (0,) * nd, pipeline_mode=pl.Buffered(1))


def _conv_layer(x, mods, ng, w_in, dw, dwb, lng, lnb, w_out, next_ret_w=None, *, i, mod_row):
    B, L, _ = x.shape
    j = i // 2
    halo = L > TILE
    n_seq = 1 if halo else SEQS_PER_STEP
    seg = SEQS_PER_STEP * TILE // n_seq
    tiles = L // seg
    hb = seg // HALO
    x_spec = pl.BlockSpec((n_seq, seg, D_MODEL), lambda b, t: (b, t, 0))
    in_specs = [x_spec]
    args = [x]
    if halo:
        in_specs += [
            pl.BlockSpec((1, HALO, D_MODEL), lambda b, t: (b, jnp.maximum(t * hb - 1, 0), 0)),
            pl.BlockSpec((1, HALO, D_MODEL), lambda b, t: (b, jnp.minimum((t + 1) * hb, L // HALO - 1), 0)),
        ]
        args += [x, x]
    mod_map = (lambda b, t: (i, mod_row + b, 0, 0)) if mod_row else (lambda b, t: (i, 0, 0, 0))
    in_specs += [
        pl.BlockSpec((None, 1, 3, D_MODEL), mod_map),
        _const_spec((1, D_MODEL), i),
        _const_spec((D_MODEL, 3 * D_INNER)),
        _const_spec((CONV_K, D_INNER), j),
        _const_spec((1, D_INNER), j),
        _const_spec((1, D_INNER), j),
        _const_spec((1, D_INNER), j),
        _const_spec((D_INNER, D_MODEL)),
    ]
    args += [mods, ng, w_in, dw, dwb, lng, lnb, w_out]
    out_specs, out_shape = [x_spec], [jax.ShapeDtypeStruct(x.shape, F32)]
    if not halo:
        job = _cast_job(next_ret_w, j, B // n_seq, lambda b, t: b)
        in_specs += job[0]
        args += list(next_ret_w)
        out_specs += job[1]
        out_shape += job[2]
    return pl.pallas_call(
        functools.partial(_conv_kernel, halo=halo, tiles_per_seq=tiles),
        grid=(B // n_seq, tiles),
        in_specs=in_specs,
        out_specs=out_specs,
        out_shape=out_shape,
        scratch_shapes=[
            pltpu.VMEM((D_INNER // LANES, n_seq, seg + 2 * HALO, LANES), F32),
            pltpu.VMEM((n_seq * seg, D_INNER), F32),
            pltpu.VMEM((n_seq * seg, D_INNER), BF16),
        ],
        compiler_params=pltpu.CompilerParams(
            dimension_semantics=("arbitrary", "arbitrary"), vmem_limit_bytes=VMEM_LIMIT),
        name="conv_sample" if halo else "conv_prompt",
    )(*args)


def _decay_mask(diff, lgf, lgb):
    e = jnp.exp(jnp.abs(diff) * jnp.where(diff >= 0.0, lgf, lgb))
    return e * jnp.where(diff == 0.0, 2.0 * RET_DK ** -0.5, RET_DK ** -0.5)


def _head_norm_gate(o, gn, g):
    mu = jnp.mean(o, axis=-1, keepdims=True)
    d = o - mu
    var = jnp.mean(d * d, axis=-1, keepdims=True)
    return (d * lax.rsqrt(var + EPS) * gn * _silu(g)).astype(BF16)


def _ret_prompt_kernel(*refs, last):
    if last:
        (x_ref, mod_ref, ng_ref, win_ref, ld_ref, gn_ref, wout_ref, fg_ref, _st_in,
         o_ref, st_ref, m_ref, w_ref, z_ref) = refs
    else:
        (x_ref, mod_ref, ng_ref, win_ref, ld_ref, gn_ref, wout_ref, cwin_ref, cwout_ref,
         o_ref, st_ref, cwin_o_ref, cwout_o_ref, m_ref, w_ref, z_ref) = refs
        cwin_o_ref[...] = cwin_ref[...].astype(BF16)
        cwout_o_ref[...] = cwout_ref[...].astype(BF16)
    L = TILE

    @pl.when(pl.program_id(0) == 0)
    def _():
        lg = -jnp.exp(ld_ref[...])
        ii = lax.broadcasted_iota(jnp.int32, (L, L), 0)
        jj = lax.broadcasted_iota(jnp.int32, (L, L), 1)
        diff = (ii - jj).astype(F32)
        pos = lax.broadcasted_iota(jnp.int32, (1, L), 1).astype(F32)
        for hd in range(RET_HEADS):
            lgf = lg[0:1, hd:hd + 1]
            lgb = lg[1:2, hd:hd + 1]
            m_ref[hd] = _decay_mask(diff, lgf, lgb)
            w_ref[2 * hd:2 * hd + 1, :] = jnp.exp((L - 1.0 - pos) * lgf) * RET_DK ** -0.5
            w_ref[2 * hd + 1:2 * hd + 2, :] = jnp.exp(pos * lgb) * RET_DK ** -0.5

    x = x_ref[0]
    h = _mod_norm(x, ng_ref[...], mod_ref).astype(BF16)
    q = _dot(h, win_ref[:, 0:D_MODEL]).astype(BF16)
    k = _dot(h, win_ref[:, D_MODEL:2 * D_MODEL])
    v = _dot(h, win_ref[:, 2 * D_MODEL:2 * D_MODEL + D_INNER]).astype(BF16)
    g = _dot(h, win_ref[:, 2 * D_MODEL + D_INNER:])
    kb = k.astype(BF16)
    for hd in range(RET_HEADS):
        kcols = slice(hd * RET_DK, (hd + 1) * RET_DK)
        vcols = slice(hd * RET_DV, (hd + 1) * RET_DV)
        vh = v[:, vcols]
        s = _dot_nt(q[:, kcols], kb[:, kcols])
        p = (s * m_ref[hd]).astype(BF16)
        o = _dot(p, vh)
        z_ref[:, vcols] = _head_norm_gate(o, gn_ref[0:1, vcols], g[:, vcols])
        kt = k[:, kcols].T
        st_ref[0, 0, 0, hd] = _dot((kt * w_ref[2 * hd:2 * hd + 1, :]).astype(BF16), vh)
        st_ref[0, 0, 1, hd] = _dot((kt * w_ref[2 * hd + 1:2 * hd + 2, :]).astype(BF16), vh)
    xn = x + mod_ref[0, 2:3, :] * _dot(z_ref[...], wout_ref[...])
    if last:
        xn = _rms(xn, fg_ref[...])
    o_ref[0] = xn


def _ret_prompt_layer(x, mods, ng, w_in, ld, gn, w_out, final_g, states, next_conv_w, *, i):
    B, L, _ = x.shape
    j = i // 2
    last = final_g is not None
    x_spec = pl.BlockSpec((1, L, D_MODEL), lambda b: (b, 0, 0))
    st_shape = (B, DEPTH // 2, 2, RET_HEADS, RET_DK, RET_DV)
    in_specs = [
        x_spec,
        pl.BlockSpec((None, 1, 3, D_MODEL), lambda b: (i, 0, 0, 0)),
        _const_spec((1, D_MODEL), i),
        _const_spec((D_MODEL, 2 * D_MODEL + 2 * D_INNER)),
        _const_spec((2, RET_HEADS), j),
        _const_spec((1, D_INNER), j),
        _const_spec((D_INNER, D_MODEL)),
    ]
    args = [x, mods, ng, w_in, ld, gn, w_out]
    aliases = {}
    if last:
        in_specs += [_const_spec((1, D_MODEL)), pl.BlockSpec(memory_space=pl.ANY)]
        args += [final_g, states]
        aliases = {len(args) - 1: 1}
    out_specs = [x_spec, pl.BlockSpec((1, 1, 2, RET_HEADS, RET_DK, RET_DV), lambda b: (b, j, 0, 0, 0, 0))]
    out_shape = [jax.ShapeDtypeStruct(x.shape, F32), jax.ShapeDtypeStruct(st_shape, F32)]
    if not last:
        job = _cast_job(next_conv_w, j + 1, B, lambda b: b)
        in_specs += job[0]
        args += list(next_conv_w)
        out_specs += job[1]
        out_shape += job[2]
    return pl.pallas_call(
        functools.partial(_ret_prompt_kernel, last=last),
        grid=(B,),
        in_specs=in_specs,
        out_specs=out_specs,
        out_shape=out_shape,
        scratch_shapes=[
            pltpu.VMEM((RET_HEADS, L, L), F32),
            pltpu.VMEM((2 * RET_HEADS, L), F32),
            pltpu.VMEM((L, D_INNER), BF16),
        ],
        input_output_aliases=aliases,
        compiler_params=pltpu.CompilerParams(
            dimension_semantics=("arbitrary",), vmem_limit_bytes=VMEM_LIMIT),
        name="ret_prompt",
    )(*args)


def _ret_sample_kernel(*refs, last):
    if last:
        (x_ref, mod_ref, ng_ref, wq_ref, wk_ref, wv_ref, wg_ref, ld_ref, gn_ref, wout_ref, s0_ref,
         cos_ref, sin_ref, fg_ref, o_ref, h_ref) = refs
    else:
        (x_ref, mod_ref, ng_ref, wq_ref, wk_ref, wv_ref, wg_ref, ld_ref, gn_ref, wout_ref, s0_ref,
         cos_ref, sin_ref, o_ref, h_ref) = refs
    L = x_ref.shape[1]
    half = RET_DK // 2
    hd = pl.program_id(1)

    @pl.when(hd == 0)
    def _():
        x = x_ref[0]
        h_ref[...] = _mod_norm(x, ng_ref[...], mod_ref).astype(BF16)
        o_ref[0] = x

    h = h_ref[...]
    cos = cos_ref[...]
    sin = sin_ref[...]

    def rope(t):
        t1, t2 = t[:, :half], t[:, half:]
        return jnp.concatenate([t1 * cos - t2 * sin, t2 * cos + t1 * sin], axis=-1)

    qh = rope(_dot(h, wq_ref[...]))
    kh = rope(_dot(h, wk_ref[...])).astype(BF16)
    vh = _dot(h, wv_ref[...]).astype(BF16)
    gh = _dot(h, wg_ref[...])
    posq = lax.broadcasted_iota(jnp.int32, (L, RET_DK), 0).astype(F32)
    lgf_q = -jnp.exp(jnp.full((1, RET_DK), ld_ref[0, hd], F32))
    lgb_q = -jnp.exp(jnp.full((1, RET_DK), ld_ref[1, hd], F32))
    ox = (_dot((qh * jnp.exp((posq + 1.0) * lgf_q)).astype(BF16), s0_ref[0, 0, 0, 0].astype(BF16))
          + _dot((qh * jnp.exp((L - posq) * lgb_q)).astype(BF16), s0_ref[0, 0, 1, 0].astype(BF16)))
    qb = qh.astype(BF16)
    gate = mod_ref[0, 2:3, :]
    n_blk = L // TILE
    ii = lax.broadcasted_iota(jnp.int32, (TILE, TILE), 0)
    jj = lax.broadcasted_iota(jnp.int32, (TILE, TILE), 1)
    diff = (ii - jj).astype(F32)
    m_diag = _decay_mask(diff, lgf_q, lgb_q)
    below = [jnp.exp((TILE + diff) * lgf_q) * RET_DK ** -0.5]
    above = [jnp.exp((TILE - diff) * lgb_q) * RET_DK ** -0.5]
    for dist in range(1, n_blk - 1):
        below.append(below[0] * jnp.exp((TILE * dist) * lgf_q))
        above.append(above[0] * jnp.exp((TILE * dist) * lgb_q))
    m = jnp.concatenate(
        [jnp.concatenate([m_diag if cb == r else below[r - cb - 1] if cb < r else above[cb - r - 1]
                          for cb in range(n_blk)], axis=1) for r in range(n_blk)], axis=0)
    p = (_dot_nt(qb, kh) * m).astype(BF16)
    o = _dot(p, vh) + ox
    z = _head_norm_gate(o, gn_ref[...], gh)
    o_ref[0] += gate * _dot(z, wout_ref[...])

    if last:
        @pl.when(hd == RET_HEADS - 1)
        def _():
            o_ref[0] = _rms(o_ref[0], fg_ref[...])


def _ret_sample_layer(x, mods, ng, w_in, ld, gn, w_out, state_ret, cos, sin, final_g, *, i):
    B, L, _ = x.shape
    j = i // 2
    last = final_g is not None
    x_spec = pl.BlockSpec((1, L, D_MODEL), lambda b, hd: (b, 0, 0))
    kb = D_MODEL // RET_DK
    vb = 2 * D_MODEL // RET_DV
    in_specs = [
        x_spec,
        pl.BlockSpec((None, 1, 3, D_MODEL), lambda b, hd: (i, 1 + b, 0, 0)),
        _const_spec((1, D_MODEL), i),
        pl.BlockSpec((D_MODEL, RET_DK), lambda b, hd: (0, hd)),
        pl.BlockSpec((D_MODEL, RET_DK), lambda b, hd: (0, kb + hd)),
        pl.BlockSpec((D_MODEL, RET_DV), lambda b, hd: (0, vb + hd)),
        pl.BlockSpec((D_MODEL, RET_DV), lambda b, hd: (0, vb + RET_HEADS + hd)),
        pl.BlockSpec(memory_space=pltpu.SMEM),
        pl.BlockSpec((None, 1, RET_DV), lambda b, hd: (j, 0, hd)),
        pl.BlockSpec((RET_DV, D_MODEL), lambda b, hd: (hd, 0)),
        pl.BlockSpec((1, 1, 2, 1, RET_DK, RET_DV), lambda b, hd: (b, j, 0, hd, 0, 0)),
        _const_spec((L, RET_DK // 2)),
        _const_spec((L, RET_DK // 2)),
    ]
    args = [x, mods, ng, w_in, w_in, w_in, w_in, ld[j], gn, w_out, state_ret, cos, sin]
    if last:
        in_specs.append(_const_spec((1, D_MODEL)))
        args.append(final_g)
    return pl.pallas_call(
        functools.partial(_ret_sample_kernel, last=last),
        grid=(B, RET_HEADS),
        in_specs=in_specs,
        out_specs=x_spec,
        out_shape=jax.ShapeDtypeStruct(x.shape, F32),
        scratch_shapes=[pltpu.VMEM((L, D_MODEL), BF16)],
        compiler_params=pltpu.CompilerParams(
            dimension_semantics=("arbitrary", "arbitrary"), vmem_limit_bytes=VMEM_LIMIT),
        name="ret_sample",
    )(*args)


def _rope_tables(L):
    rows = L // GRID_W
    r = jnp.repeat(jnp.arange(rows, dtype=F32), GRID_W)
    col = jnp.tile(jnp.arange(GRID_W, dtype=F32), rows)
    nf = RET_DK // 4
    inv = ROPE_BASE ** (-jnp.arange(nf, dtype=F32) / nf)
    ang = jnp.concatenate([r[:, None] * inv, col[:, None] * inv], axis=-1)
    return jnp.cos(ang), jnp.sin(ang)


def kernel(x_prompt, x_sample, c, state_ret, c_ctx, ada_w, ada_b, norm_g, final_norm_g, conv_w_in, conv_dw, conv_dw_b, conv_ln_g, conv_ln_b, conv_w_out, ret_w_in, ret_log_decay, ret_gn_g, ret_w_out):
    n_dec = c.shape[0]
    cond8 = jnp.concatenate([c_ctx[None, :], c, jnp.zeros((8 - 1 - n_dec, D_MODEL), F32)], axis=0)
    mods = _adaln(cond8, ada_w, ada_b).reshape(DEPTH, 8, 3, D_MODEL)
    cos, sin = _rope_tables(x_sample.shape[1])
    fg = final_norm_g.reshape(1, D_MODEL)

    ng = norm_g.reshape(DEPTH, 1, D_MODEL)
    conv_rest = (conv_dw, conv_dw_b[:, None, :], conv_ln_g[:, None, :], conv_ln_b[:, None, :])
    gn = ret_gn_g[:, None, :]

    xp, xs = x_prompt, x_sample
    states = None
    cw_in, cw_out = _cast_conv_weights(conv_w_in, conv_w_out, 0)
    for i in range(DEPTH):
        if i % 2 == 0:
            xp, rw_in, rw_out = _conv_layer(xp, mods, ng, cw_in, *conv_rest, cw_out, (ret_w_in, ret_w_out),
                                            i=i, mod_row=0)
            xs, = _conv_layer(xs, mods, ng, cw_in, *conv_rest, cw_out, i=i, mod_row=1)
        else:
            final_g = fg if i == DEPTH - 1 else None
            res = _ret_prompt_layer(xp, mods, ng, rw_in, ret_log_decay, gn, rw_out, final_g, states,
                                    (conv_w_in, conv_w_out), i=i)
            xp, states = res[0], res[1]
            if final_g is None:
                cw_in, cw_out = res[2], res[3]
            xs = _ret_sample_layer(xs, mods, ng, rw_in, ret_log_decay, gn, rw_out, state_ret, cos, sin,
                                   final_g, i=i)
    return (xp, xs, states)
```

```python
import functools

import jax
import jax.numpy as jnp
from jax import lax
from jax.experimental import pallas as pl
from jax.experimental.pallas import tpu as pltpu

D_MODEL = 1024
D_INNER = 2048
DEPTH = 4
CONV_K = 31
HALO = 16
RET_HEADS = 4
RET_DK = 256
RET_DV = 512
GRID_W = 64
ROPE_BASE = 10000.0
EPS = 1e-6
TILE = 256
LANES = 128
SUBLANES = 8
CHUNK = 256
SEQS_PER_STEP = 2
VMEM_LIMIT = 56 * 1024 * 1024

F32 = jnp.float32
BF16 = jnp.bfloat16


def _dot(a, b):
    return jnp.dot(a, b, preferred_element_type=F32)


def _dot_nt(a, b):
    return lax.dot_general(a, b, (((1,), (1,)), ((), ())), preferred_element_type=F32)


def _sigmoid(x):
    return 1.0 / (1.0 + jnp.exp(-x))


def _silu(x):
    return x * _sigmoid(x)


def _rms(x, g):
    return x * lax.rsqrt(jnp.mean(x * x, axis=-1, keepdims=True) + EPS) * g


def _mod_norm(x, g, mod_ref):
    r = lax.rsqrt(jnp.mean(x * x, axis=-1, keepdims=True) + EPS)
    return x * r * (g * (1.0 + mod_ref[0, 1:2, :])) + mod_ref[0, 0:1, :]


def _adaln_kernel(cond_ref, w_ref, b_ref, o_ref):
    s = _silu(cond_ref[...]).astype(BF16)
    o_ref[0] = _dot(s, w_ref[0].astype(BF16)) + b_ref[0]


def _adaln(cond8, ada_w, ada_b):
    return pl.pallas_call(
        _adaln_kernel,
        grid=(DEPTH,),
        in_specs=[
            pl.BlockSpec((SUBLANES, D_MODEL), lambda i: (0, 0)),
            pl.BlockSpec((1, D_MODEL, 3 * D_MODEL), lambda i: (i, 0, 0)),
            pl.BlockSpec((1, 1, 3 * D_MODEL), lambda i: (i, 0, 0)),
        ],
        out_specs=pl.BlockSpec((1, SUBLANES, 3 * D_MODEL), lambda i: (i, 0, 0)),
        out_shape=jax.ShapeDtypeStruct((DEPTH, SUBLANES, 3 * D_MODEL), F32),
        compiler_params=pltpu.CompilerParams(
            dimension_semantics=("arbitrary",), vmem_limit_bytes=VMEM_LIMIT),
        name="adaln",
    )(cond8, ada_w, ada_b.reshape(DEPTH, 1, 3 * D_MODEL))


def _cast_job(ws, layer, steps, step_of):
    in_specs, out_specs, out_shape = [], [], []
    for w in ws:
        rows, cols = w.shape[1] // steps, w.shape[2]
        in_specs.append(pl.BlockSpec((None, rows, cols), lambda *g: (layer, step_of(*g), 0)))
        out_specs.append(pl.BlockSpec((rows, cols), lambda *g: (step_of(*g), 0)))
        out_shape.append(jax.ShapeDtypeStruct(w.shape[1:], BF16))
    return in_specs, out_specs, out_shape


def _cast_kernel(a_ref, b_ref, ao_ref, bo_ref):
    ao_ref[...] = a_ref[...].astype(BF16)
    bo_ref[...] = b_ref[...].astype(BF16)


def _cast_conv_weights(w_in, w_out, j, steps=8):
    specs_in, specs_out, shapes = _cast_job((w_in, w_out), j, steps, lambda s: s)
    return pl.pallas_call(
        _cast_kernel, grid=(steps,), in_specs=specs_in, out_specs=specs_out, out_shape=shapes,
        compiler_params=pltpu.CompilerParams(dimension_semantics=("arbitrary",), vmem_limit_bytes=VMEM_LIMIT),
        name="cast_conv_weights",
    )(w_in, w_out)


def _exact_zero_after(v):
    bits = pltpu.bitcast(v, jnp.uint32)
    return pltpu.bitcast((bits >> 16) >> 16, F32)


def _conv_kernel(*refs, halo, tiles_per_seq):
    if halo:
        (x_ref, xt_ref, xb_ref, mod_ref, ng_ref, win_ref, dw_ref, dwb_ref, lng_ref, lnb_ref,
         wout_ref, o_ref, pad_ref, u_ref, gate_ref) = refs
    else:
        (x_ref, mod_ref, ng_ref, win_ref, dw_ref, dwb_ref, lng_ref, lnb_ref, wout_ref, rwin_ref, rwout_ref,
         o_ref, rwin_o_ref, rwout_o_ref, pad_ref, u_ref, gate_ref) = refs
        rwin_o_ref[...] = rwin_ref[...].astype(BF16)
        rwout_o_ref[...] = rwout_ref[...].astype(BF16)
    n_seq, seg = x_ref.shape[0], x_ref.shape[1]
    n_slab = CHUNK // LANES
    n_chunk = D_INNER // CHUNK
    ng = ng_ref[...]
    h = _mod_norm(x_ref[...].reshape(n_seq * seg, D_MODEL), ng, mod_ref).astype(BF16)
    if halo:
        t = pl.program_id(1)
        ht = _mod_norm(xt_ref[0], ng, mod_ref).astype(BF16)
        hb = _mod_norm(xb_ref[0], ng, mod_ref).astype(BF16)
        hext = jnp.concatenate([ht, h, hb], axis=0)
        row = lax.broadcasted_iota(jnp.int32, (seg + 2 * HALO, CHUNK), 0)
        valid = jnp.logical_and(jnp.logical_or(row >= HALO, t > 0),
                                jnp.logical_or(row < HALO + seg, t < tiles_per_seq - 1))
    else:
        hext = h
        zeros = jnp.zeros((HALO, LANES), F32)
        for s in range(D_INNER // LANES):
            for q in range(n_seq):
                pad_ref[s, q, 0:HALO, :] = zeros
                pad_ref[s, q, HALO + seg:2 * HALO + seg, :] = zeros

    def finish(f):
        rows = slice(f * TILE, (f + 1) * TILE)
        q, r0 = divmod(f * TILE, seg)
        u = u_ref[rows, :]
        mu = jnp.mean(u, axis=-1, keepdims=True)
        d = u - mu
        var = jnp.mean(d * d, axis=-1, keepdims=True)
        y = d * lax.rsqrt(var + EPS) * lng_ref[...] + lnb_ref[...]
        z = _silu(y.astype(BF16)) * gate_ref[rows, :]
        out = x_ref[q, r0:r0 + TILE, :] + mod_ref[0, 2:3, :] * _dot(z, wout_ref[...])
        o_ref[q, r0:r0 + TILE, :] = out
        return out

    n_fin = n_seq * seg // TILE
    pin = None
    for c in range(n_chunk):
        c0 = c * CHUNK
        a = _dot(hext, win_ref[:, c0:c0 + CHUNK])
        b = _dot(hext, win_ref[:, D_INNER + c0:D_INNER + c0 + CHUNK])
        g = _dot(h, win_ref[:, 2 * D_INNER + c0:2 * D_INNER + c0 + CHUNK])
        u = a * _sigmoid(b)
        gate_ref[:, c0:c0 + CHUNK] = _silu(g.astype(BF16))
        if halo:
            u = jnp.where(valid, u, 0.0)
        for l in range(n_slab):
            s = c * n_slab + l
            if halo:
                pad_ref[s, 0, :, :] = u[:, l * LANES:(l + 1) * LANES]
            else:
                for q in range(n_seq):
                    pad_ref[s, q, HALO:HALO + seg, :] = u[q * seg:(q + 1) * seg, l * LANES:(l + 1) * LANES]
        for f in range(n_fin):
            if f > 0 and c == n_chunk - 1:
                pin = _exact_zero_after(finish(f - 1)[TILE - SUBLANES:TILE, D_MODEL - LANES:D_MODEL])[0:1, :]
            for l in range(n_slab):
                s = c * n_slab + l
                lanes = slice(c0 + l * LANES, c0 + (l + 1) * LANES)
                bias = dwb_ref[0:1, lanes]
                if f > 0 and c == n_chunk - 1:
                    bias = bias + pin
                for rb in range(TILE // LANES):
                    q, r0 = divmod(f * TILE + rb * LANES, seg)
                    acc = jnp.broadcast_to(bias, (LANES, LANES))
                    for k in range(CONV_K):
                        start = r0 + k + HALO - CONV_K // 2
                        acc = acc + pad_ref[s, q, pl.ds(start, LANES), :] * dw_ref[k:k + 1, lanes]
                    u_ref[f * TILE + rb * LANES:f * TILE + (rb + 1) * LANES, lanes] = acc
    finish(n_fin - 1)


def _const_spec(shape, layer=None):
    nd = len(shape)
    if layer is None:
        return pl.BlockSpec(shape, lambda *_: (0,) * nd, pipeline_mode=pl.Buffered(1))
    return pl.BlockSpec((None,) + shape, lambda *_: (layer,) + (0,) * nd, pipeline_mode=pl.Buffered(1))


def _conv_layer(x, mods, ng, w_in, dw, dwb, lng, lnb, w_out, next_ret_w=None, *, i, mod_row):
    B, L, _ = x.shape
    j = i // 2
    halo = L > TILE
    n_seq = 1 if halo else SEQS_PER_STEP
    seg = SEQS_PER_STEP * TILE // n_seq
    tiles = L // seg
    hb = seg // HALO
    x_spec = pl.BlockSpec((n_seq, seg, D_MODEL), lambda b, t: (b, t, 0))
    in_specs = [x_spec]
    args = [x]
    if halo:
        in_specs += [
            pl.BlockSpec((1, HALO, D_MODEL), lambda b, t: (b, jnp.maximum(t * hb - 1, 0), 0)),
            pl.BlockSpec((1, HALO, D_MODEL), lambda b, t: (b, jnp.minimum((t + 1) * hb, L // HALO - 1), 0)),
        ]
        args += [x, x]
    mod_map = (lambda b, t: (i, mod_row + b, 0, 0)) if mod_row else (lambda b, t: (i, 0, 0, 0))
    in_specs += [
        pl.BlockSpec((None, 1, 3, D_MODEL), mod_map),
        _const_spec((1, D_MODEL), i),
        _const_spec((D_MODEL, 3 * D_INNER)),
        _const_spec((CONV_K, D_INNER), j),
        _const_spec((1, D_INNER), j),
        _const_spec((1, D_INNER), j),
        _const_spec((1, D_INNER), j),
        _const_spec((D_INNER, D_MODEL)),
    ]
    args += [mods, ng, w_in, dw, dwb, lng, lnb, w_out]
    out_specs, out_shape = [x_spec], [jax.ShapeDtypeStruct(x.shape, F32)]
    if not halo:
        job = _cast_job(next_ret_w, j, B // n_seq, lambda b, t: b)
        in_specs += job[0]
        args += list(next_ret_w)
        out_specs += job[1]
        out_shape += job[2]
    return pl.pallas_call(
        functools.partial(_conv_kernel, halo=halo, tiles_per_seq=tiles),
        grid=(B // n_seq, tiles),
        in_specs=in_specs,
        out_specs=out_specs,
        out_shape=out_shape,
        scratch_shapes=[
            pltpu.VMEM((D_INNER // LANES, n_seq, seg + 2 * HALO, LANES), F32),
            pltpu.VMEM((n_seq * seg, D_INNER), F32),
            pltpu.VMEM((n_seq * seg, D_INNER), BF16),
        ],
        compiler_params=pltpu.CompilerParams(
            dimension_semantics=("arbitrary", "arbitrary"), vmem_limit_bytes=VMEM_LIMIT),
        name="conv_sample" if halo else "conv_prompt",
    )(*args)


def _decay_mask(diff, lgf, lgb):
    e = jnp.exp(jnp.abs(diff) * jnp.where(diff >= 0.0, lgf, lgb))
    return e * jnp.where(diff == 0.0, 2.0 * RET_DK ** -0.5, RET_DK ** -0.5)


def _head_norm_gate(o, gn, g):
    mu = jnp.mean(o, axis=-1, keepdims=True)
    d = o - mu
    var = jnp.mean(d * d, axis=-1, keepdims=True)
    return (d * lax.rsqrt(var + EPS) * gn * _silu(g)).astype(BF16)


def _ret_prompt_kernel(*refs, last):
    if last:
        (x_ref, mod_ref, ng_ref, win_ref, ld_ref, gn_ref, wout_ref, fg_ref, _st_in,
         o_ref, st_ref, m_ref, w_ref, z_ref) = refs
    else:
        (x_ref, mod_ref, ng_ref, win_ref, ld_ref, gn_ref, wout_ref, cwin_ref, cwout_ref,
         o_ref, st_ref, cwin_o_ref, cwout_o_ref, m_ref, w_ref, z_ref) = refs
        cwin_o_ref[...] = cwin_ref[...].astype(BF16)
        cwout_o_ref[...] = cwout_ref[...].astype(BF16)
    L = TILE

    @pl.when(pl.program_id(0) == 0)
    def _():
        lg = -jnp.exp(ld_ref[...])
        ii = lax.broadcasted_iota(jnp.int32, (L, L), 0)
        jj = lax.broadcasted_iota(jnp.int32, (L, L), 1)
        diff = (ii - jj).astype(F32)
        pos = lax.broadcasted_iota(jnp.int32, (1, L), 1).astype(F32)
        for hd in range(RET_HEADS):
            lgf = lg[0:1, hd:hd + 1]
            lgb = lg[1:2, hd:hd + 1]
            m_ref[hd] = _decay_mask(diff, lgf, lgb)
            w_ref[2 * hd:2 * hd + 1, :] = jnp.exp((L - 1.0 - pos) * lgf) * RET_DK ** -0.5
            w_ref[2 * hd + 1:2 * hd + 2, :] = jnp.exp(pos * lgb) * RET_DK ** -0.5

    x = x_ref[0]
    h = _mod_norm(x, ng_ref[...], mod_ref).astype(BF16)
    q = _dot(h, win_ref[:, 0:D_MODEL]).astype(BF16)
    k = _dot(h, win_ref[:, D_MODEL:2 * D_MODEL])
    v = _dot(h, win_ref[:, 2 * D_MODEL:2 * D_MODEL + D_INNER]).astype(BF16)
    g = _dot(h, win_ref[:, 2 * D_MODEL + D_INNER:])
    kb = k.astype(BF16)
    for hd in range(RET_HEADS):
        kcols = slice(hd * RET_DK, (hd + 1) * RET_DK)
        vcols = slice(hd * RET_DV, (hd + 1) * RET_DV)
        vh = v[:, vcols]
        s = _dot_nt(q[:, kcols], kb[:, kcols])
        p = (s * m_ref[hd]).astype(BF16)
        o = _dot(p, vh)
        z_ref[:, vcols] = _head_norm_gate(o, gn_ref[0:1, vcols], g[:, vcols])
        kt = k[:, kcols].T
        ktw = jnp.concatenate([(kt * w_ref[2 * hd:2 * hd + 1, :]).astype(BF16),
                               (kt * w_ref[2 * hd + 1:2 * hd + 2, :]).astype(BF16)], axis=0)
        st = _dot(ktw, vh)
        st_ref[0, 0, 0, hd] = st[:RET_DK]
        st_ref[0, 0, 1, hd] = st[RET_DK:]
    xn = x + mod_ref[0, 2:3, :] * _dot(z_ref[...], wout_ref[...])
    if last:
        xn = _rms(xn, fg_ref[...])
    o_ref[0] = xn


def _ret_prompt_layer(x, mods, ng, w_in, ld, gn, w_out, final_g, states, next_conv_w, *, i):
    B, L, _ = x.shape
    j = i // 2
    last = final_g is not None
    x_spec = pl.BlockSpec((1, L, D_MODEL), lambda b: (b, 0, 0))
    st_shape = (B, DEPTH // 2, 2, RET_HEADS, RET_DK, RET_DV)
    in_specs = [
        x_spec,
        pl.BlockSpec((None, 1, 3, D_MODEL), lambda b: (i, 0, 0, 0)),
        _const_spec((1, D_MODEL), i),
        _const_spec((D_MODEL, 2 * D_MODEL + 2 * D_INNER)),
        _const_spec((2, RET_HEADS), j),
        _const_spec((1, D_INNER), j),
        _const_spec((D_INNER, D_MODEL)),
    ]
    args = [x, mods, ng, w_in, ld, gn, w_out]
    aliases = {}
    if last:
        in_specs += [_const_spec((1, D_MODEL)), pl.BlockSpec(memory_space=pl.ANY)]
        args += [final_g, states]
        aliases = {len(args) - 1: 1}
    out_specs = [x_spec, pl.BlockSpec((1, 1, 2, RET_HEADS, RET_DK, RET_DV), lambda b: (b, j, 0, 0, 0, 0))]
    out_shape = [jax.ShapeDtypeStruct(x.shape, F32), jax.ShapeDtypeStruct(st_shape, F32)]
    if not last:
        job = _cast_job(next_conv_w, j + 1, B, lambda b: b)
        in_specs += job[0]
        args += list(next_conv_w)
        out_specs += job[1]
        out_shape += job[2]
    return pl.pallas_call(
        functools.partial(_ret_prompt_kernel, last=last),
        grid=(B,),
        in_specs=in_specs,
        out_specs=out_specs,
        out_shape=out_shape,
        scratch_shapes=[
            pltpu.VMEM((RET_HEADS, L, L), F32),
            pltpu.VMEM((2 * RET_HEADS, L), F32),
            pltpu.VMEM((L, D_INNER), BF16),
        ],
        input_output_aliases=aliases,
        compiler_params=pltpu.CompilerParams(
            dimension_semantics=("arbitrary",), vmem_limit_bytes=VMEM_LIMIT),
        name="ret_prompt",
    )(*args)


def _ret_sample_kernel(*refs, last):
    if last:
        (x_ref, mod_ref, ng_ref, wq_ref, wk_ref, wv_ref, wg_ref, ld_ref, gn_ref, wout_ref, s0_ref,
         cos_ref, sin_ref, fg_ref, o_ref, h_ref) = refs
    else:
        (x_ref, mod_ref, ng_ref, wq_ref, wk_ref, wv_ref, wg_ref, ld_ref, gn_ref, wout_ref, s0_ref,
         cos_ref, sin_ref, o_ref, h_ref) = refs
    L = x_ref.shape[1]
    half = RET_DK // 2
    hd = pl.program_id(1)

    @pl.when(hd == 0)
    def _():
        x = x_ref[0]
        h_ref[...] = _mod_norm(x, ng_ref[...], mod_ref).astype(BF16)
        o_ref[0] = x

    h = h_ref[...]
    cos = cos_ref[...]
    sin = sin_ref[...]

    def rope(t):
        t1, t2 = t[:, :half], t[:, half:]
        return jnp.concatenate([t1 * cos - t2 * sin, t2 * cos + t1 * sin], axis=-1)

    qh = rope(_dot(h, wq_ref[...]))
    kh = rope(_dot(h, wk_ref[...])).astype(BF16)
    vh = _dot(h, wv_ref[...]).astype(BF16)
    gh = _dot(h, wg_ref[...])
    posq = lax.broadcasted_iota(jnp.int32, (L, RET_DK), 0).astype(F32)
    lgf_q = -jnp.exp(jnp.full((1, RET_DK), ld_ref[0, hd], F32))
    lgb_q = -jnp.exp(jnp.full((1, RET_DK), ld_ref[1, hd], F32))
    ox = (_dot((qh * jnp.exp((posq + 1.0) * lgf_q)).astype(BF16), s0_ref[0, 0, 0, 0].astype(BF16))
          + _dot((qh * jnp.exp((L - posq) * lgb_q)).astype(BF16), s0_ref[0, 0, 1, 0].astype(BF16)))
    qb = qh.astype(BF16)
    gate = mod_ref[0, 2:3, :]
    n_blk = L // TILE
    ii = lax.broadcasted_iota(jnp.int32, (TILE, TILE), 0)
    jj = lax.broadcasted_iota(jnp.int32, (TILE, TILE), 1)
    diff = (ii - jj).astype(F32)
    m_diag = _decay_mask(diff, lgf_q, lgb_q)
    below = [jnp.exp((TILE + diff) * lgf_q) * RET_DK ** -0.5]
    above = [jnp.exp((TILE - diff) * lgb_q) * RET_DK ** -0.5]
    for dist in range(1, n_blk - 1):
        below.append(below[0] * jnp.exp((TILE * dist) * lgf_q))
        above.append(above[0] * jnp.exp((TILE * dist) * lgb_q))
    m = jnp.concatenate(
        [jnp.concatenate([m_diag if cb == r else below[r - cb - 1] if cb < r else above[cb - r - 1]
                          for cb in range(n_blk)], axis=1) for r in range(n_blk)], axis=0)
    p = (_dot_nt(qb, kh) * m).astype(BF16)
    o = _dot(p, vh) + ox
    z = _head_norm_gate(o, gn_ref[...], gh)
    o_ref[0] += gate * _dot(z, wout_ref[...])

    if last:
        @pl.when(hd == RET_HEADS - 1)
        def _():
            o_ref[0] = _rms(o_ref[0], fg_ref[...])


def _ret_sample_layer(x, mods, ng, w_in, ld, gn, w_out, state_ret, cos, sin, final_g, *, i):
    B, L, _ = x.shape
    j = i // 2
    last = final_g is not None
    x_spec = pl.BlockSpec((1, L, D_MODEL), lambda b, hd: (b, 0, 0))
    kb = D_MODEL // RET_DK
    vb = 2 * D_MODEL // RET_DV
    in_specs = [
        x_spec,
        pl.BlockSpec((None, 1, 3, D_MODEL), lambda b, hd: (i, 1 + b, 0, 0)),
        _const_spec((1, D_MODEL), i),
        pl.BlockSpec((D_MODEL, RET_DK), lambda b, hd: (0, hd)),
        pl.BlockSpec((D_MODEL, RET_DK), lambda b, hd: (0, kb + hd)),
        pl.BlockSpec((D_MODEL, RET_DV), lambda b, hd: (0, vb + hd)),
        pl.BlockSpec((D_MODEL, RET_DV), lambda b, hd: (0, vb + RET_HEADS + hd)),
        pl.BlockSpec(memory_space=pltpu.SMEM),
        pl.BlockSpec((None, 1, RET_DV), lambda b, hd: (j, 0, hd)),
        pl.BlockSpec((RET_DV, D_MODEL), lambda b, hd: (hd, 0)),
        pl.BlockSpec((1, 1, 2, 1, RET_DK, RET_DV), lambda b, hd: (b, j, 0, hd, 0, 0)),
        _const_spec((L, RET_DK // 2)),
        _const_spec((L, RET_DK // 2)),
    ]
    args = [x, mods, ng, w_in, w_in, w_in, w_in, ld[j], gn, w_out, state_ret, cos, sin]
    if last:
        in_specs.append(_const_spec((1, D_MODEL)))
        args.append(final_g)
    return pl.pallas_call(
        functools.partial(_ret_sample_kernel, last=last),
        grid=(B, RET_HEADS),
        in_specs=in_specs,
        out_specs=x_spec,
        out_shape=jax.ShapeDtypeStruct(x.shape, F32),
        scratch_shapes=[pltpu.VMEM((L, D_MODEL), BF16)],
        compiler_params=pltpu.CompilerParams(
            dimension_semantics=("arbitrary", "arbitrary"), vmem_limit_bytes=VMEM_LIMIT),
        name="ret_sample",
    )(*args)


def _rope_tables(L):
    rows = L // GRID_W
    r = jnp.repeat(jnp.arange(rows, dtype=F32), GRID_W)
    col = jnp.tile(jnp.arange(GRID_W, dtype=F32), rows)
    nf = RET_DK // 4
    inv = ROPE_BASE ** (-jnp.arange(nf, dtype=F32) / nf)
    ang = jnp.concatenate([r[:, None] * inv, col[:, None] * inv], axis=-1)
    return jnp.cos(ang), jnp.sin(ang)


def kernel(x_prompt, x_sample, c, state_ret, c_ctx, ada_w, ada_b, norm_g, final_norm_g, conv_w_in, conv_dw, conv_dw_b, conv_ln_g, conv_ln_b, conv_w_out, ret_w_in, ret_log_decay, ret_gn_g, ret_w_out):
    n_dec = c.shape[0]
    cond8 = jnp.concatenate([c_ctx[None, :], c, jnp.zeros((SUBLANES - 1 - n_dec, D_MODEL), F32)], axis=0)
    mods = _adaln(cond8, ada_w, ada_b).reshape(DEPTH, SUBLANES, 3, D_MODEL)
    cos, sin = _rope_tables(x_sample.shape[1])
    fg = final_norm_g.reshape(1, D_MODEL)

    ng = norm_g.reshape(DEPTH, 1, D_MODEL)
    conv_rest = (conv_dw, conv_dw_b[:, None, :], conv_ln_g[:, None, :], conv_ln_b[:, None, :])
    gn = ret_gn_g[:, None, :]

    xp, xs = x_prompt, x_sample
    states = None
    cw_in, cw_out = _cast_conv_weights(conv_w_in, conv_w_out, 0)
    for i in range(DEPTH):
        if i % 2 == 0:
            xp, rw_in, rw_out = _conv_layer(xp, mods, ng, cw_in, *conv_rest, cw_out, (ret_w_in, ret_w_out),
                                            i=i, mod_row=0)
            xs, = _conv_layer(xs, mods, ng, cw_in, *conv_rest, cw_out, i=i, mod_row=1)
        else:
            final_g = fg if i == DEPTH - 1 else None
            res = _ret_prompt_layer(xp, mods, ng, rw_in, ret_log_decay, gn, rw_out, final_g, states,
                                    (conv_w_in, conv_w_out), i=i)
            xp, states = res[0], res[1]
            if final_g is None:
                cw_in, cw_out = res[2], res[3]
            xs = _ret_sample_layer(xs, mods, ng, rw_in, ret_log_decay, gn, rw_out, state_ret, cos, sin,
                                   final_g, i=i)
    return (xp, xs, states)
```

```python
import functools

import jax
import jax.numpy as jnp
import numpy as np
from jax import lax
from jax.experimental import pallas as pl
from jax.experimental.pallas import tpu as pltpu

D_MODEL = 1024
D_INNER = 2048
DEPTH = 4
CONV_K = 31
HALO = 16
RET_HEADS = 4
RET_DK = 256
RET_DV = 512
GRID_W = 64
ROPE_BASE = 10000.0
EPS = 1e-6
TILE = 256
LANES = 128
SUBLANES = 8
CHUNK = 256
SEQS_PER_STEP = 2
VMEM_LIMIT = 56 * 1024 * 1024

F32 = jnp.float32
BF16 = jnp.bfloat16


def _dot(a, b):
    return jnp.dot(a, b, preferred_element_type=F32)


def _dot_nt(a, b):
    return lax.dot_general(a, b, (((1,), (1,)), ((), ())), preferred_element_type=F32)


def _sigmoid(x):
    return 1.0 / (1.0 + jnp.exp(-x))


def _silu(x):
    return x * _sigmoid(x)


def _rms(x, g):
    return x * lax.rsqrt(jnp.mean(x * x, axis=-1, keepdims=True) + EPS) * g


def _mod_norm(x, g, mod_ref):
    r = lax.rsqrt(jnp.mean(x * x, axis=-1, keepdims=True) + EPS)
    return x * r * (g * (1.0 + mod_ref[0, 1:2, :])) + mod_ref[0, 0:1, :]


def _adaln_kernel(cond_ref, w_ref, b_ref, o_ref):
    s = _silu(cond_ref[...]).astype(BF16)
    o_ref[0] = _dot(s, w_ref[0].astype(BF16)) + b_ref[0]


def _adaln(cond8, ada_w, ada_b):
    return pl.pallas_call(
        _adaln_kernel,
        grid=(DEPTH,),
        in_specs=[
            pl.BlockSpec((SUBLANES, D_MODEL), lambda i: (0, 0)),
            pl.BlockSpec((1, D_MODEL, 3 * D_MODEL), lambda i: (i, 0, 0)),
            pl.BlockSpec((1, 1, 3 * D_MODEL), lambda i: (i, 0, 0)),
        ],
        out_specs=pl.BlockSpec((1, SUBLANES, 3 * D_MODEL), lambda i: (i, 0, 0)),
        out_shape=jax.ShapeDtypeStruct((DEPTH, SUBLANES, 3 * D_MODEL), F32),
        compiler_params=pltpu.CompilerParams(
            dimension_semantics=("arbitrary",), vmem_limit_bytes=VMEM_LIMIT),
        name="adaln",
    )(cond8, ada_w, ada_b.reshape(DEPTH, 1, 3 * D_MODEL))


def _cast_job(ws, layer, steps, step_of):
    in_specs, out_specs, out_shape = [], [], []
    for w in ws:
        rows, cols = w.shape[1] // steps, w.shape[2]
        in_specs.append(pl.BlockSpec((None, rows, cols), lambda *g: (layer, step_of(*g), 0)))
        out_specs.append(pl.BlockSpec((rows, cols), lambda *g: (step_of(*g), 0)))
        out_shape.append(jax.ShapeDtypeStruct(w.shape[1:], BF16))
    return in_specs, out_specs, out_shape


def _cast_kernel(a_ref, b_ref, ao_ref, bo_ref):
    ao_ref[...] = a_ref[...].astype(BF16)
    bo_ref[...] = b_ref[...].astype(BF16)


def _cast_conv_weights(w_in, w_out, j, steps=8):
    specs_in, specs_out, shapes = _cast_job((w_in, w_out), j, steps, lambda s: s)
    return pl.pallas_call(
        _cast_kernel, grid=(steps,), in_specs=specs_in, out_specs=specs_out, out_shape=shapes,
        compiler_params=pltpu.CompilerParams(dimension_semantics=("arbitrary",), vmem_limit_bytes=VMEM_LIMIT),
        name="cast_conv_weights",
    )(w_in, w_out)


def _exact_zero_after(v):
    bits = pltpu.bitcast(v, jnp.uint32)
    return pltpu.bitcast((bits >> 16) >> 16, F32)


def _conv_kernel(*refs, halo, tiles_per_seq, layer):
    if halo:
        (x_ref, xt_ref, xb_ref, mod_ref, ng_ref, win_ref, dw_ref, dwb_ref, lng_ref, lnb_ref,
         wout_ref, o_ref, pad_ref, u_ref, gate_ref) = refs
    else:
        (x_ref, mod_ref, ng_ref, win_ref, dw_ref, dwb_ref, lng_ref, lnb_ref, wout_ref, rwin_ref, rwout_ref,
         o_ref, rwin_o_ref, rwout_o_ref, pad_ref, u_ref, gate_ref) = refs
        rwin_o_ref[...] = rwin_ref[...].astype(BF16)
        rwout_o_ref[...] = rwout_ref[...].astype(BF16)
    n_seq, seg = x_ref.shape[0], x_ref.shape[1]
    n_slab = CHUNK // LANES
    n_chunk = D_INNER // CHUNK
    j = layer // 2
    ng = ng_ref[layer:layer + 1, :]
    h = _mod_norm(x_ref[...].reshape(n_seq * seg, D_MODEL), ng, mod_ref).astype(BF16)
    if halo:
        t = pl.program_id(1)
        ht = _mod_norm(xt_ref[0], ng, mod_ref).astype(BF16)
        hb = _mod_norm(xb_ref[0], ng, mod_ref).astype(BF16)
        hext = jnp.concatenate([ht, h, hb], axis=0)
        row = lax.broadcasted_iota(jnp.int32, (seg + 2 * HALO, CHUNK), 0)
        valid = jnp.logical_and(jnp.logical_or(row >= HALO, t > 0),
                                jnp.logical_or(row < HALO + seg, t < tiles_per_seq - 1))
    else:
        hext = h
        zeros = jnp.zeros((HALO, LANES), F32)
        for s in range(D_INNER // LANES):
            for q in range(n_seq):
                pad_ref[s, q, 0:HALO, :] = zeros
                pad_ref[s, q, HALO + seg:2 * HALO + seg, :] = zeros

    def finish(f):
        rows = slice(f * TILE, (f + 1) * TILE)
        q, r0 = divmod(f * TILE, seg)
        u = u_ref[rows, :]
        mu = jnp.mean(u, axis=-1, keepdims=True)
        d = u - mu
        var = jnp.mean(d * d, axis=-1, keepdims=True)
        y = d * lax.rsqrt(var + EPS) * lng_ref[j:j + 1, :] + lnb_ref[j:j + 1, :]
        z = _silu(y.astype(BF16)) * gate_ref[rows, :]
        out = x_ref[q, r0:r0 + TILE, :] + mod_ref[0, 2:3, :] * _dot(z, wout_ref[...])
        o_ref[q, r0:r0 + TILE, :] = out
        return out

    n_fin = n_seq * seg // TILE
    pin = None
    for c in range(n_chunk):
        c0 = c * CHUNK
        a = _dot(hext, win_ref[:, c0:c0 + CHUNK])
        b = _dot(hext, win_ref[:, D_INNER + c0:D_INNER + c0 + CHUNK])
        g = _dot(h, win_ref[:, 2 * D_INNER + c0:2 * D_INNER + c0 + CHUNK])
        u = a * _sigmoid(b)
        gate_ref[:, c0:c0 + CHUNK] = _silu(g.astype(BF16))
        if halo:
            u = jnp.where(valid, u, 0.0)
        for l in range(n_slab):
            s = c * n_slab + l
            if halo:
                pad_ref[s, 0, :, :] = u[:, l * LANES:(l + 1) * LANES]
            else:
                for q in range(n_seq):
                    pad_ref[s, q, HALO:HALO + seg, :] = u[q * seg:(q + 1) * seg, l * LANES:(l + 1) * LANES]
        for f in range(n_fin):
            if f > 0 and c == n_chunk - 1:
                pin = _exact_zero_after(finish(f - 1)[TILE - SUBLANES:TILE, D_MODEL - LANES:D_MODEL])[0:1, :]
            for l in range(n_slab):
                s = c * n_slab + l
                lanes = slice(c0 + l * LANES, c0 + (l + 1) * LANES)
                bias = dwb_ref[j:j + 1, lanes]
                if f > 0 and c == n_chunk - 1:
                    bias = bias + pin
                for rb in range(TILE // LANES):
                    q, r0 = divmod(f * TILE + rb * LANES, seg)
                    acc = jnp.broadcast_to(bias, (LANES, LANES))
                    for k in range(CONV_K):
                        start = r0 + k + HALO - CONV_K // 2
                        acc = acc + pad_ref[s, q, pl.ds(start, LANES), :] * dw_ref[k:k + 1, lanes]
                    u_ref[f * TILE + rb * LANES:f * TILE + (rb + 1) * LANES, lanes] = acc
    finish(n_fin - 1)


def _const_spec(shape, layer=None):
    nd = len(shape)
    if layer is None:
        return pl.BlockSpec(shape, lambda *_: (0,) * nd, pipeline_mode=pl.Buffered(1))
    return pl.BlockSpec((None,) + shape, lambda *_: (layer,) + (0,) * nd, pipeline_mode=pl.Buffered(1))


def _conv_layer(x, mods, ng, w_in, dw, dwb, lng, lnb, w_out, next_ret_w=None, *, i, mod_row):
    B, L, _ = x.shape
    j = i // 2
    halo = L > TILE
    n_seq = 1 if halo else SEQS_PER_STEP
    seg = SEQS_PER_STEP * TILE // n_seq
    tiles = L // seg
    hb = seg // HALO
    x_spec = pl.BlockSpec((n_seq, seg, D_MODEL), lambda b, t: (b, t, 0))
    in_specs = [x_spec]
    args = [x]
    if halo:
        in_specs += [
            pl.BlockSpec((1, HALO, D_MODEL), lambda b, t: (b, jnp.maximum(t * hb - 1, 0), 0)),
            pl.BlockSpec((1, HALO, D_MODEL), lambda b, t: (b, jnp.minimum((t + 1) * hb, L // HALO - 1), 0)),
        ]
        args += [x, x]
    mod_map = (lambda b, t: (i, mod_row + b, 0, 0)) if mod_row else (lambda b, t: (i, 0, 0, 0))
    in_specs += [
        pl.BlockSpec((None, 1, 3, D_MODEL), mod_map),
        _const_spec(ng.shape),
        _const_spec((D_MODEL, 3 * D_INNER)),
        _const_spec((CONV_K, D_INNER), j),
        _const_spec(dwb.shape),
        _const_spec(lng.shape),
        _const_spec(lnb.shape),
        _const_spec((D_INNER, D_MODEL)),
    ]
    args += [mods, ng, w_in, dw, dwb, lng, lnb, w_out]
    out_specs, out_shape = [x_spec], [jax.ShapeDtypeStruct(x.shape, F32)]
    if not halo:
        job = _cast_job(next_ret_w, j, B // n_seq, lambda b, t: b)
        in_specs += job[0]
        args += list(next_ret_w)
        out_specs += job[1]
        out_shape += job[2]
    return pl.pallas_call(
        functools.partial(_conv_kernel, halo=halo, tiles_per_seq=tiles, layer=i),
        grid=(B // n_seq, tiles),
        in_specs=in_specs,
        out_specs=out_specs,
        out_shape=out_shape,
        scratch_shapes=[
            pltpu.VMEM((D_INNER // LANES, n_seq, seg + 2 * HALO, LANES), F32),
            pltpu.VMEM((n_seq * seg, D_INNER), F32),
            pltpu.VMEM((n_seq * seg, D_INNER), BF16),
        ],
        compiler_params=pltpu.CompilerParams(
            dimension_semantics=("arbitrary", "arbitrary"), vmem_limit_bytes=VMEM_LIMIT),
        name="conv_sample" if halo else "conv_prompt",
    )(*args)


def _decay_mask(diff, lgf, lgb):
    e = jnp.exp(jnp.abs(diff) * jnp.where(diff >= 0.0, lgf, lgb))
    return e * jnp.where(diff == 0.0, 2.0 * RET_DK ** -0.5, RET_DK ** -0.5)


def _head_norm_gate(o, gn, g):
    mu = jnp.mean(o, axis=-1, keepdims=True)
    d = o - mu
    var = jnp.mean(d * d, axis=-1, keepdims=True)
    return (d * lax.rsqrt(var + EPS) * gn * _silu(g)).astype(BF16)


def _ret_prompt_kernel(*refs, last, layer):
    if last:
        (x_ref, mod_ref, ng_ref, win_ref, ld_ref, gn_ref, wout_ref, fg_ref, _st_in,
         o_ref, st_ref, m_ref, w_ref, z_ref) = refs
    else:
        (x_ref, mod_ref, ng_ref, win_ref, ld_ref, gn_ref, wout_ref, cwin_ref, cwout_ref,
         o_ref, st_ref, cwin_o_ref, cwout_o_ref, m_ref, w_ref, z_ref) = refs
        cwin_o_ref[...] = cwin_ref[...].astype(BF16)
        cwout_o_ref[...] = cwout_ref[...].astype(BF16)
    L = TILE

    @pl.when(pl.program_id(0) == 0)
    def _():
        lg = -jnp.exp(ld_ref[...])
        ii = lax.broadcasted_iota(jnp.int32, (L, L), 0)
        jj = lax.broadcasted_iota(jnp.int32, (L, L), 1)
        diff = (ii - jj).astype(F32)
        pos = lax.broadcasted_iota(jnp.int32, (1, L), 1).astype(F32)
        for hd in range(RET_HEADS):
            lgf = lg[0:1, hd:hd + 1]
            lgb = lg[1:2, hd:hd + 1]
            m_ref[hd] = _decay_mask(diff, lgf, lgb)
            w_ref[2 * hd:2 * hd + 1, :] = jnp.exp((L - 1.0 - pos) * lgf) * RET_DK ** -0.5
            w_ref[2 * hd + 1:2 * hd + 2, :] = jnp.exp(pos * lgb) * RET_DK ** -0.5

    x = x_ref[0]
    j = layer // 2
    h = _mod_norm(x, ng_ref[layer:layer + 1, :], mod_ref).astype(BF16)
    q = _dot(h, win_ref[:, 0:D_MODEL]).astype(BF16)
    k = _dot(h, win_ref[:, D_MODEL:2 * D_MODEL])
    v = _dot(h, win_ref[:, 2 * D_MODEL:2 * D_MODEL + D_INNER]).astype(BF16)
    g = _dot(h, win_ref[:, 2 * D_MODEL + D_INNER:])
    kb = k.astype(BF16)
    for hd in range(RET_HEADS):
        kcols = slice(hd * RET_DK, (hd + 1) * RET_DK)
        vcols = slice(hd * RET_DV, (hd + 1) * RET_DV)
        vh = v[:, vcols]
        s = _dot_nt(q[:, kcols], kb[:, kcols])
        p = (s * m_ref[hd]).astype(BF16)
        o = _dot(p, vh)
        z_ref[:, vcols] = _head_norm_gate(o, gn_ref[j:j + 1, vcols], g[:, vcols])
        kt = k[:, kcols].T
        ktw = jnp.concatenate([(kt * w_ref[2 * hd:2 * hd + 1, :]).astype(BF16),
                               (kt * w_ref[2 * hd + 1:2 * hd + 2, :]).astype(BF16)], axis=0)
        st = _dot(ktw, vh)
        st_ref[0, 0, 0, hd] = st[:RET_DK]
        st_ref[0, 0, 1, hd] = st[RET_DK:]
    xn = x + mod_ref[0, 2:3, :] * _dot(z_ref[...], wout_ref[...])
    if last:
        xn = _rms(xn, fg_ref[...])
    o_ref[0] = xn


def _ret_prompt_layer(x, mods, ng, w_in, ld, gn, w_out, final_g, states, next_conv_w, *, i):
    B, L, _ = x.shape
    j = i // 2
    last = final_g is not None
    x_spec = pl.BlockSpec((1, L, D_MODEL), lambda b: (b, 0, 0))
    st_shape = (B, DEPTH // 2, 2, RET_HEADS, RET_DK, RET_DV)
    in_specs = [
        x_spec,
        pl.BlockSpec((None, 1, 3, D_MODEL), lambda b: (i, 0, 0, 0)),
        _const_spec(ng.shape),
        _const_spec((D_MODEL, 2 * D_MODEL + 2 * D_INNER)),
        _const_spec((2, RET_HEADS), j),
        _const_spec(gn.shape),
        _const_spec((D_INNER, D_MODEL)),
    ]
    args = [x, mods, ng, w_in, ld, gn, w_out]
    aliases = {}
    if last:
        in_specs += [_const_spec((1, D_MODEL)), pl.BlockSpec(memory_space=pl.ANY)]
        args += [final_g, states]
        aliases = {len(args) - 1: 1}
    out_specs = [x_spec, pl.BlockSpec((1, 1, 2, RET_HEADS, RET_DK, RET_DV), lambda b: (b, j, 0, 0, 0, 0))]
    out_shape = [jax.ShapeDtypeStruct(x.shape, F32), jax.ShapeDtypeStruct(st_shape, F32)]
    if not last:
        job = _cast_job(next_conv_w, j + 1, B, lambda b: b)
        in_specs += job[0]
        args += list(next_conv_w)
        out_specs += job[1]
        out_shape += job[2]
    return pl.pallas_call(
        functools.partial(_ret_prompt_kernel, last=last, layer=i),
        grid=(B,),
        in_specs=in_specs,
        out_specs=out_specs,
        out_shape=out_shape,
        scratch_shapes=[
            pltpu.VMEM((RET_HEADS, L, L), F32),
            pltpu.VMEM((2 * RET_HEADS, L), F32),
            pltpu.VMEM((L, D_INNER), BF16),
        ],
        input_output_aliases=aliases,
        compiler_params=pltpu.CompilerParams(
            dimension_semantics=("arbitrary",), vmem_limit_bytes=VMEM_LIMIT),
        name="ret_prompt",
    )(*args)


def _ret_sample_kernel(*refs, last, layer):
    if last:
        (x_ref, mod_ref, ng_ref, wq_ref, wk_ref, wv_ref, wg_ref, ld_ref, gn_ref, wout_ref, s0_ref,
         cos_ref, sin_ref, fg_ref, o_ref, h_ref) = refs
    else:
        (x_ref, mod_ref, ng_ref, wq_ref, wk_ref, wv_ref, wg_ref, ld_ref, gn_ref, wout_ref, s0_ref,
         cos_ref, sin_ref, o_ref, h_ref) = refs
    L = x_ref.shape[1]
    half = RET_DK // 2
    hd = pl.program_id(1)

    @pl.when(hd == 0)
    def _():
        x = x_ref[0]
        h_ref[...] = _mod_norm(x, ng_ref[layer:layer + 1, :], mod_ref).astype(BF16)
        o_ref[0] = x

    h = h_ref[...]
    cos = cos_ref[...]
    sin = sin_ref[...]

    def rope(t):
        t1, t2 = t[:, :half], t[:, half:]
        return jnp.concatenate([t1 * cos - t2 * sin, t2 * cos + t1 * sin], axis=-1)

    qh = rope(_dot(h, wq_ref[...]))
    kh = rope(_dot(h, wk_ref[...])).astype(BF16)
    vh = _dot(h, wv_ref[...]).astype(BF16)
    gh = _dot(h, wg_ref[...])
    posq = lax.broadcasted_iota(jnp.int32, (L, RET_DK), 0).astype(F32)
    lgf_q = -jnp.exp(jnp.full((1, RET_DK), ld_ref[0, hd], F32))
    lgb_q = -jnp.exp(jnp.full((1, RET_DK), ld_ref[1, hd], F32))
    ox = (_dot((qh * jnp.exp((posq + 1.0) * lgf_q)).astype(BF16), s0_ref[0, 0, 0, 0].astype(BF16))
          + _dot((qh * jnp.exp((L - posq) * lgb_q)).astype(BF16), s0_ref[0, 0, 1, 0].astype(BF16)))
    qb = qh.astype(BF16)
    gate = mod_ref[0, 2:3, :]
    n_blk = L // TILE
    ii = lax.broadcasted_iota(jnp.int32, (TILE, TILE), 0)
    jj = lax.broadcasted_iota(jnp.int32, (TILE, TILE), 1)
    diff = (ii - jj).astype(F32)
    m_diag = _decay_mask(diff, lgf_q, lgb_q)
    below = [jnp.exp((TILE + diff) * lgf_q) * RET_DK ** -0.5]
    above = [jnp.exp((TILE - diff) * lgb_q) * RET_DK ** -0.5]
    for dist in range(1, n_blk - 1):
        below.append(below[0] * jnp.exp((TILE * dist) * lgf_q))
        above.append(above[0] * jnp.exp((TILE * dist) * lgb_q))
    m = jnp.concatenate(
        [jnp.concatenate([m_diag if cb == r else below[r - cb - 1] if cb < r else above[cb - r - 1]
                          for cb in range(n_blk)], axis=1) for r in range(n_blk)], axis=0)
    p = (_dot_nt(qb, kh) * m).astype(BF16)
    o = _dot(p, vh) + ox
    z = _head_norm_gate(o, gn_ref[layer // 2:layer // 2 + 1, :], gh)
    o_ref[0] += gate * _dot(z, wout_ref[...])

    if last:
        @pl.when(hd == RET_HEADS - 1)
        def _():
            o_ref[0] = _rms(o_ref[0], fg_ref[...])


def _ret_sample_layer(x, mods, ng, w_in, ld, gn, w_out, state_ret, cos, sin, final_g, *, i):
    B, L, _ = x.shape
    j = i // 2
    last = final_g is not None
    x_spec = pl.BlockSpec((1, L, D_MODEL), lambda b, hd: (b, 0, 0))
    kb = D_MODEL // RET_DK
    vb = 2 * D_MODEL // RET_DV
    in_specs = [
        x_spec,
        pl.BlockSpec((None, 1, 3, D_MODEL), lambda b, hd: (i, 1 + b, 0, 0)),
        _const_spec(ng.shape),
        pl.BlockSpec((D_MODEL, RET_DK), lambda b, hd: (0, hd)),
        pl.BlockSpec((D_MODEL, RET_DK), lambda b, hd: (0, kb + hd)),
        pl.BlockSpec((D_MODEL, RET_DV), lambda b, hd: (0, vb + hd)),
        pl.BlockSpec((D_MODEL, RET_DV), lambda b, hd: (0, vb + RET_HEADS + hd)),
        pl.BlockSpec(memory_space=pltpu.SMEM),
        pl.BlockSpec((gn.shape[0], RET_DV), lambda b, hd: (0, hd)),
        pl.BlockSpec((RET_DV, D_MODEL), lambda b, hd: (hd, 0)),
        pl.BlockSpec((1, 1, 2, 1, RET_DK, RET_DV), lambda b, hd: (b, j, 0, hd, 0, 0)),
        _const_spec((L, RET_DK // 2)),
        _const_spec((L, RET_DK // 2)),
    ]
    args = [x, mods, ng, w_in, w_in, w_in, w_in, ld[j], gn, w_out, state_ret, cos, sin]
    if last:
        in_specs.append(_const_spec((1, D_MODEL)))
        args.append(final_g)
    return pl.pallas_call(
        functools.partial(_ret_sample_kernel, last=last, layer=i),
        grid=(B, RET_HEADS),
        in_specs=in_specs,
        out_specs=x_spec,
        out_shape=jax.ShapeDtypeStruct(x.shape, F32),
        scratch_shapes=[pltpu.VMEM((L, D_MODEL), BF16)],
        compiler_params=pltpu.CompilerParams(
            dimension_semantics=("arbitrary", "arbitrary"), vmem_limit_bytes=VMEM_LIMIT),
        name="ret_sample",
    )(*args)


def _rope_tables(L):
    rows = L // GRID_W
    r = np.repeat(np.arange(rows, dtype=np.float32), GRID_W)
    col = np.tile(np.arange(GRID_W, dtype=np.float32), rows)
    nf = RET_DK // 4
    inv = (ROPE_BASE ** (-np.arange(nf, dtype=np.float32) / nf)).astype(np.float32)
    ang = np.concatenate([r[:, None] * inv, col[:, None] * inv], axis=-1).astype(np.float32)
    return jnp.asarray(np.cos(ang), F32), jnp.asarray(np.sin(ang), F32)


def kernel(x_prompt, x_sample, c, state_ret, c_ctx, ada_w, ada_b, norm_g, final_norm_g, conv_w_in, conv_dw, conv_dw_b, conv_ln_g, conv_ln_b, conv_w_out, ret_w_in, ret_log_decay, ret_gn_g, ret_w_out):
    n_dec = c.shape[0]
    cond8 = jnp.concatenate([c_ctx[None, :], c, jnp.zeros((SUBLANES - 1 - n_dec, D_MODEL), F32)], axis=0)
    mods = _adaln(cond8, ada_w, ada_b).reshape(DEPTH, SUBLANES, 3, D_MODEL)
    cos, sin = _rope_tables(x_sample.shape[1])
    fg = final_norm_g.reshape(1, D_MODEL)

    ng = norm_g
    conv_rest = (conv_dw, conv_dw_b, conv_ln_g, conv_ln_b)
    gn = ret_gn_g

    xp, xs = x_prompt, x_sample
    states = None
    cw_in, cw_out = _cast_conv_weights(conv_w_in, conv_w_out, 0)
    for i in range(DEPTH):
        if i % 2 == 0:
            xp, rw_in, rw_out = _conv_layer(xp, mods, ng, cw_in, *conv_rest, cw_out, (ret_w_in, ret_w_out),
                                            i=i, mod_row=0)
            xs, = _conv_layer(xs, mods, ng, cw_in, *conv_rest, cw_out, i=i, mod_row=1)
        else:
            final_g = fg if i == DEPTH - 1 else None
            res = _ret_prompt_layer(xp, mods, ng, rw_in, ret_log_decay, gn, rw_out, final_g, states,
                                    (conv_w_in, conv_w_out), i=i)
            xp, states = res[0], res[1]
            if final_g is None:
                cw_in, cw_out = res[2], res[3]
            xs = _ret_sample_layer(xs, mods, ng, rw_in, ret_log_decay, gn, rw_out, state_ret, cos, sin,
                                   final_g, i=i)
    return (xp, xs, states)
```

```python
import functools

import jax
import jax.numpy as jnp
import numpy as np
from jax import lax
from jax.experimental import pallas as pl
from jax.experimental.pallas import tpu as pltpu

D_MODEL = 1024
D_INNER = 2048
DEPTH = 4
CONV_K = 31
HALO = 16
RET_HEADS = 4
RET_DK = 256
RET_DV = 512
GRID_W = 64
ROPE_BASE = 10000.0
EPS = 1e-6
TILE = 256
LANES = 128
SUBLANES = 8
CHUNK = 256
SEQS_PER_STEP = 2
VMEM_LIMIT = 56 * 1024 * 1024

F32 = jnp.float32
BF16 = jnp.bfloat16


def _dot(a, b):
    return jnp.dot(a, b, preferred_element_type=F32)


def _dot_nt(a, b):
    return lax.dot_general(a, b, (((1,), (1,)), ((), ())), preferred_element_type=F32)


def _sigmoid(x):
    return 1.0 / (1.0 + jnp.exp(-x))


def _silu(x):
    return x * _sigmoid(x)


def _rms(x, g):
    return x * lax.rsqrt(jnp.mean(x * x, axis=-1, keepdims=True) + EPS) * g


def _mod_norm(x, g, mod_ref):
    r = lax.rsqrt(jnp.mean(x * x, axis=-1, keepdims=True) + EPS)
    return x * r * (g * (1.0 + mod_ref[0, 1:2, :])) + mod_ref[0, 0:1, :]


def _adaln_kernel(cond_ref, w_ref, b_ref, o_ref):
    s = _silu(cond_ref[...]).astype(BF16)
    o_ref[0] = _dot(s, w_ref[0].astype(BF16)) + b_ref[pl.ds(pl.program_id(0), 1), :]


def _adaln(cond8, ada_w, ada_b):
    return pl.pallas_call(
        _adaln_kernel,
        grid=(DEPTH,),
        in_specs=[
            pl.BlockSpec((SUBLANES, D_MODEL), lambda i: (0, 0)),
            pl.BlockSpec((1, D_MODEL, 3 * D_MODEL), lambda i: (i, 0, 0)),
            pl.BlockSpec((DEPTH, 3 * D_MODEL), lambda i: (0, 0)),
        ],
        out_specs=pl.BlockSpec((1, SUBLANES, 3 * D_MODEL), lambda i: (i, 0, 0)),
        out_shape=jax.ShapeDtypeStruct((DEPTH, SUBLANES, 3 * D_MODEL), F32),
        compiler_params=pltpu.CompilerParams(
            dimension_semantics=("arbitrary",), vmem_limit_bytes=VMEM_LIMIT),
        name="adaln",
    )(cond8, ada_w, ada_b)


def _cast_job(ws, layer, steps, step_of):
    in_specs, out_specs, out_shape = [], [], []
    for w in ws:
        rows, cols = w.shape[1] // steps, w.shape[2]
        in_specs.append(pl.BlockSpec((None, rows, cols), lambda *g: (layer, step_of(*g), 0)))
        out_specs.append(pl.BlockSpec((rows, cols), lambda *g: (step_of(*g), 0)))
        out_shape.append(jax.ShapeDtypeStruct(w.shape[1:], BF16))
    return in_specs, out_specs, out_shape


def _cast_kernel(a_ref, b_ref, ao_ref, bo_ref):
    ao_ref[...] = a_ref[...].astype(BF16)
    bo_ref[...] = b_ref[...].astype(BF16)


def _cast_conv_weights(w_in, w_out, j, steps=8):
    specs_in, specs_out, shapes = _cast_job((w_in, w_out), j, steps, lambda s: s)
    return pl.pallas_call(
        _cast_kernel, grid=(steps,), in_specs=specs_in, out_specs=specs_out, out_shape=shapes,
        compiler_params=pltpu.CompilerParams(dimension_semantics=("arbitrary",), vmem_limit_bytes=VMEM_LIMIT),
        name="cast_conv_weights",
    )(w_in, w_out)


def _exact_zero_after(v):
    bits = pltpu.bitcast(v, jnp.uint32)
    return pltpu.bitcast((bits >> 16) >> 16, F32)


def _conv_kernel(*refs, halo, tiles_per_seq, layer):
    if halo:
        (x_ref, xt_ref, xb_ref, mod_ref, ng_ref, win_ref, dw_ref, dwb_ref, lng_ref, lnb_ref,
         wout_ref, o_ref, pad_ref, u_ref, gate_ref) = refs
    else:
        (x_ref, mod_ref, ng_ref, win_ref, dw_ref, dwb_ref, lng_ref, lnb_ref, wout_ref, rwin_ref, rwout_ref,
         o_ref, rwin_o_ref, rwout_o_ref, pad_ref, u_ref, gate_ref) = refs
        rwin_o_ref[...] = rwin_ref[...].astype(BF16)
        rwout_o_ref[...] = rwout_ref[...].astype(BF16)
    n_seq, seg = x_ref.shape[0], x_ref.shape[1]
    n_slab = CHUNK // LANES
    n_chunk = D_INNER // CHUNK
    j = layer // 2
    ng = ng_ref[layer:layer + 1, :]
    h = _mod_norm(x_ref[...].reshape(n_seq * seg, D_MODEL), ng, mod_ref).astype(BF16)
    if halo:
        t = pl.program_id(1)
        ht = _mod_norm(xt_ref[0], ng, mod_ref).astype(BF16)
        hb = _mod_norm(xb_ref[0], ng, mod_ref).astype(BF16)
        hext = jnp.concatenate([ht, h, hb], axis=0)
        row = lax.broadcasted_iota(jnp.int32, (seg + 2 * HALO, CHUNK), 0)
        valid = jnp.logical_and(jnp.logical_or(row >= HALO, t > 0),
                                jnp.logical_or(row < HALO + seg, t < tiles_per_seq - 1))
    else:
        hext = h
        zeros = jnp.zeros((HALO, LANES), F32)
        for s in range(D_INNER // LANES):
            for q in range(n_seq):
                pad_ref[s, q, 0:HALO, :] = zeros
                pad_ref[s, q, HALO + seg:2 * HALO + seg, :] = zeros

    def finish(f):
        rows = slice(f * TILE, (f + 1) * TILE)
        q, r0 = divmod(f * TILE, seg)
        u = u_ref[rows, :]
        mu = jnp.mean(u, axis=-1, keepdims=True)
        d = u - mu
        var = jnp.mean(d * d, axis=-1, keepdims=True)
        y = d * lax.rsqrt(var + EPS) * lng_ref[j:j + 1, :] + lnb_ref[j:j + 1, :]
        z = _silu(y.astype(BF16)) * gate_ref[rows, :]
        out = x_ref[q, r0:r0 + TILE, :] + mod_ref[0, 2:3, :] * _dot(z, wout_ref[...])
        o_ref[q, r0:r0 + TILE, :] = out
        return out

    n_fin = n_seq * seg // TILE
    pin = None
    for c in range(n_chunk):
        c0 = c * CHUNK
        a = _dot(hext, win_ref[:, c0:c0 + CHUNK])
        b = _dot(hext, win_ref[:, D_INNER + c0:D_INNER + c0 + CHUNK])
        g = _dot(h, win_ref[:, 2 * D_INNER + c0:2 * D_INNER + c0 + CHUNK])
        u = a * _sigmoid(b)
        gate_ref[:, c0:c0 + CHUNK] = _silu(g.astype(BF16))
        if halo:
            u = jnp.where(valid, u, 0.0)
        for l in range(n_slab):
            s = c * n_slab + l
            if halo:
                pad_ref[s, 0, :, :] = u[:, l * LANES:(l + 1) * LANES]
            else:
                for q in range(n_seq):
                    pad_ref[s, q, HALO:HALO + seg, :] = u[q * seg:(q + 1) * seg, l * LANES:(l + 1) * LANES]
        for f in range(n_fin):
            if f > 0 and c == n_chunk - 1:
                pin = _exact_zero_after(finish(f - 1)[TILE - SUBLANES:TILE, D_MODEL - LANES:D_MODEL])[0:1, :]
            for l in range(n_slab):
                s = c * n_slab + l
                lanes = slice(c0 + l * LANES, c0 + (l + 1) * LANES)
                bias = dwb_ref[j:j + 1, lanes]
                if f > 0 and c == n_chunk - 1:
                    bias = bias + pin
                for rb in range(TILE // LANES):
                    q, r0 = divmod(f * TILE + rb * LANES, seg)
                    acc = jnp.broadcast_to(bias, (LANES, LANES))
                    for k in range(CONV_K):
                        start = r0 + k + HALO - CONV_K // 2
                        acc = acc + pad_ref[s, q, pl.ds(start, LANES), :] * dw_ref[k:k + 1, lanes]
                    u_ref[f * TILE + rb * LANES:f * TILE + (rb + 1) * LANES, lanes] = acc
    finish(n_fin - 1)


def _const_spec(shape, layer=None):
    nd = len(shape)
    if layer is None:
        return pl.BlockSpec(shape, lambda *_: (0,) * nd, pipeline_mode=pl.Buffered(1))
    return pl.BlockSpec((None,) + shape, lambda *_: (layer,) + (0,) * nd, pipeline_mode=pl.Buffered(1))


def _conv_layer(x, mods, ng, w_in, dw, dwb, lng, lnb, w_out, next_ret_w=None, *, i, mod_row):
    B, L, _ = x.shape
    j = i // 2
    halo = L > TILE
    n_seq = 1 if halo else SEQS_PER_STEP
    seg = SEQS_PER_STEP * TILE // n_seq
    tiles = L // seg
    hb = seg // HALO
    x_spec = pl.BlockSpec((n_seq, seg, D_MODEL), lambda b, t: (b, t, 0))
    in_specs = [x_spec]
    args = [x]
    if halo:
        in_specs += [
            pl.BlockSpec((1, HALO, D_MODEL), lambda b, t: (b, jnp.maximum(t * hb - 1, 0), 0)),
            pl.BlockSpec((1, HALO, D_MODEL), lambda b, t: (b, jnp.minimum((t + 1) * hb, L // HALO - 1), 0)),
        ]
        args += [x, x]
    mod_map = (lambda b, t: (i, mod_row + b, 0, 0)) if mod_row else (lambda b, t: (i, 0, 0, 0))
    in_specs += [
        pl.BlockSpec((None, 1, 3, D_MODEL), mod_map),
        _const_spec(ng.shape),
        _const_spec((D_MODEL, 3 * D_INNER)),
        _const_spec((CONV_K, D_INNER), j),
        _const_spec(dwb.shape),
        _const_spec(lng.shape),
        _const_spec(lnb.shape),
        _const_spec((D_INNER, D_MODEL)),
    ]
    args += [mods, ng, w_in, dw, dwb, lng, lnb, w_out]
    out_specs, out_shape = [x_spec], [jax.ShapeDtypeStruct(x.shape, F32)]
    if not halo:
        job = _cast_job(next_ret_w, j, B // n_seq, lambda b, t: b)
        in_specs += job[0]
        args += list(next_ret_w)
        out_specs += job[1]
        out_shape += job[2]
    return pl.pallas_call(
        functools.partial(_conv_kernel, halo=halo, tiles_per_seq=tiles, layer=i),
        grid=(B // n_seq, tiles),
        in_specs=in_specs,
        out_specs=out_specs,
        out_shape=out_shape,
        scratch_shapes=[
            pltpu.VMEM((D_INNER // LANES, n_seq, seg + 2 * HALO, LANES), F32),
            pltpu.VMEM((n_seq * seg, D_INNER), F32),
            pltpu.VMEM((n_seq * seg, D_INNER), BF16),
        ],
        compiler_params=pltpu.CompilerParams(
            dimension_semantics=("arbitrary", "arbitrary"), vmem_limit_bytes=VMEM_LIMIT),
        name="conv_sample" if halo else "conv_prompt",
    )(*args)


def _decay_mask(diff, lgf, lgb):
    e = jnp.exp(jnp.abs(diff) * jnp.where(diff >= 0.0, lgf, lgb))
    return e * jnp.where(diff == 0.0, 2.0 * RET_DK ** -0.5, RET_DK ** -0.5)


def _head_norm_gate(o, gn, g):
    mu = jnp.mean(o, axis=-1, keepdims=True)
    d = o - mu
    var = jnp.mean(d * d, axis=-1, keepdims=True)
    return (d * lax.rsqrt(var + EPS) * gn * _silu(g)).astype(BF16)


def _ret_prompt_kernel(*refs, last, layer):
    if last:
        (x_ref, mod_ref, ng_ref, win_ref, ld_ref, gn_ref, wout_ref, fg_ref, _st_in,
         o_ref, st_ref, m_ref, w_ref, z_ref) = refs
    else:
        (x_ref, mod_ref, ng_ref, win_ref, ld_ref, gn_ref, wout_ref, cwin_ref, cwout_ref,
         o_ref, st_ref, cwin_o_ref, cwout_o_ref, m_ref, w_ref, z_ref) = refs
        cwin_o_ref[...] = cwin_ref[...].astype(BF16)
        cwout_o_ref[...] = cwout_ref[...].astype(BF16)
    L = TILE

    @pl.when(pl.program_id(0) == 0)
    def _():
        lg = -jnp.exp(ld_ref[...])
        ii = lax.broadcasted_iota(jnp.int32, (L, L), 0)
        jj = lax.broadcasted_iota(jnp.int32, (L, L), 1)
        diff = (ii - jj).astype(F32)
        pos = lax.broadcasted_iota(jnp.int32, (1, L), 1).astype(F32)
        for hd in range(RET_HEADS):
            lgf = lg[0:1, hd:hd + 1]
            lgb = lg[1:2, hd:hd + 1]
            m_ref[hd] = _decay_mask(diff, lgf, lgb)
            w_ref[2 * hd:2 * hd + 1, :] = jnp.exp((L - 1.0 - pos) * lgf) * RET_DK ** -0.5
            w_ref[2 * hd + 1:2 * hd + 2, :] = jnp.exp(pos * lgb) * RET_DK ** -0.5

    x = x_ref[0]
    j = layer // 2
    h = _mod_norm(x, ng_ref[layer:layer + 1, :], mod_ref).astype(BF16)
    q = _dot(h, win_ref[:, 0:D_MODEL]).astype(BF16)
    k = _dot(h, win_ref[:, D_MODEL:2 * D_MODEL])
    v = _dot(h, win_ref[:, 2 * D_MODEL:2 * D_MODEL + D_INNER]).astype(BF16)
    g = _dot(h, win_ref[:, 2 * D_MODEL + D_INNER:])
    kb = k.astype(BF16)
    for hd in range(RET_HEADS):
        kcols = slice(hd * RET_DK, (hd + 1) * RET_DK)
        vcols = slice(hd * RET_DV, (hd + 1) * RET_DV)
        vh = v[:, vcols]
        s = _dot_nt(q[:, kcols], kb[:, kcols])
        p = (s * m_ref[hd]).astype(BF16)
        o = _dot(p, vh)
        z_ref[:, vcols] = _head_norm_gate(o, gn_ref[j:j + 1, vcols], g[:, vcols])
        kt = k[:, kcols].T
        ktw = jnp.concatenate([(kt * w_ref[2 * hd:2 * hd + 1, :]).astype(BF16),
                               (kt * w_ref[2 * hd + 1:2 * hd + 2, :]).astype(BF16)], axis=0)
        st = _dot(ktw, vh)
        st_ref[0, 0, 0, hd] = st[:RET_DK]
        st_ref[0, 0, 1, hd] = st[RET_DK:]
    xn = x + mod_ref[0, 2:3, :] * _dot(z_ref[...], wout_ref[...])
    if last:
        xn = _rms(xn, fg_ref[...])
    o_ref[0] = xn


def _ret_prompt_layer(x, mods, ng, w_in, ld, gn, w_out, final_g, states, next_conv_w, *, i):
    B, L, _ = x.shape
    j = i // 2
    last = final_g is not None
    x_spec = pl.BlockSpec((1, L, D_MODEL), lambda b: (b, 0, 0))
    st_shape = (B, DEPTH // 2, 2, RET_HEADS, RET_DK, RET_DV)
    in_specs = [
        x_spec,
        pl.BlockSpec((None, 1, 3, D_MODEL), lambda b: (i, 0, 0, 0)),
        _const_spec(ng.shape),
        _const_spec((D_MODEL, 2 * D_MODEL + 2 * D_INNER)),
        _const_spec((2, RET_HEADS), j),
        _const_spec(gn.shape),
        _const_spec((D_INNER, D_MODEL)),
    ]
    args = [x, mods, ng, w_in, ld, gn, w_out]
    aliases = {}
    if last:
        in_specs += [_const_spec((1, D_MODEL)), pl.BlockSpec(memory_space=pl.ANY)]
        args += [final_g, states]
        aliases = {len(args) - 1: 1}
    out_specs = [x_spec, pl.BlockSpec((1, 1, 2, RET_HEADS, RET_DK, RET_DV), lambda b: (b, j, 0, 0, 0, 0))]
    out_shape = [jax.ShapeDtypeStruct(x.shape, F32), jax.ShapeDtypeStruct(st_shape, F32)]
    if not last:
        job = _cast_job(next_conv_w, j + 1, B, lambda b: b)
        in_specs += job[0]
        args += list(next_conv_w)
        out_specs += job[1]
        out_shape += job[2]
    return pl.pallas_call(
        functools.partial(_ret_prompt_kernel, last=last, layer=i),
        grid=(B,),
        in_specs=in_specs,
        out_specs=out_specs,
        out_shape=out_shape,
        scratch_shapes=[
            pltpu.VMEM((RET_HEADS, L, L), F32),
            pltpu.VMEM((2 * RET_HEADS, L), F32),
            pltpu.VMEM((L, D_INNER), BF16),
        ],
        input_output_aliases=aliases,
        compiler_params=pltpu.CompilerParams(
            dimension_semantics=("arbitrary",), vmem_limit_bytes=VMEM_LIMIT),
        name="ret_prompt",
    )(*args)


def _ret_sample_kernel(*refs, last, layer):
    if last:
        (x_ref, mod_ref, ng_ref, wq_ref, wk_ref, wv_ref, wg_ref, ld_ref, gn_ref, wout_ref, s0_ref,
         cos_ref, sin_ref, fg_ref, o_ref, h_ref) = refs
    else:
        (x_ref, mod_ref, ng_ref, wq_ref, wk_ref, wv_ref, wg_ref, ld_ref, gn_ref, wout_ref, s0_ref,
         cos_ref, sin_ref, o_ref, h_ref) = refs
    L = x_ref.shape[1]
    half = RET_DK // 2
    hd = pl.program_id(1)

    @pl.when(hd == 0)
    def _():
        x = x_ref[0]
        h_ref[...] = _mod_norm(x, ng_ref[layer:layer + 1, :], mod_ref).astype(BF16)
        o_ref[0] = x

    h = h_ref[...]
    cos = cos_ref[...]
    sin = sin_ref[...]

    def rope(t):
        t1, t2 = t[:, :half], t[:, half:]
        return jnp.concatenate([t1 * cos - t2 * sin, t2 * cos + t1 * sin], axis=-1)

    qh = rope(_dot(h, wq_ref[...]))
    kh = rope(_dot(h, wk_ref[...])).astype(BF16)
    vh = _dot(h, wv_ref[...]).astype(BF16)
    gh = _dot(h, wg_ref[...])
    posq = lax.broadcasted_iota(jnp.int32, (L, RET_DK), 0).astype(F32)
    lgf_q = -jnp.exp(jnp.full((1, RET_DK), ld_ref[layer // 2, 0, hd], F32))
    lgb_q = -jnp.exp(jnp.full((1, RET_DK), ld_ref[layer // 2, 1, hd], F32))
    ox = (_dot((qh * jnp.exp((posq + 1.0) * lgf_q)).astype(BF16), s0_ref[0, 0, 0, 0].astype(BF16))
          + _dot((qh * jnp.exp((L - posq) * lgb_q)).astype(BF16), s0_ref[0, 0, 1, 0].astype(BF16)))
    qb = qh.astype(BF16)
    gate = mod_ref[0, 2:3, :]
    n_blk = L // TILE
    ii = lax.broadcasted_iota(jnp.int32, (TILE, TILE), 0)
    jj = lax.broadcasted_iota(jnp.int32, (TILE, TILE), 1)
    diff = (ii - jj).astype(F32)
    m_diag = _decay_mask(diff, lgf_q, lgb_q)
    below = [jnp.exp((TILE + diff) * lgf_q) * RET_DK ** -0.5]
    above = [jnp.exp((TILE - diff) * lgb_q) * RET_DK ** -0.5]
    for dist in range(1, n_blk - 1):
        below.append(below[0] * jnp.exp((TILE * dist) * lgf_q))
        above.append(above[0] * jnp.exp((TILE * dist) * lgb_q))
    m = jnp.concatenate(
        [jnp.concatenate([m_diag if cb == r else below[r - cb - 1] if cb < r else above[cb - r - 1]
                          for cb in range(n_blk)], axis=1) for r in range(n_blk)], axis=0)
    p = (_dot_nt(qb, kh) * m).astype(BF16)
    o = _dot(p, vh) + ox
    z = _head_norm_gate(o, gn_ref[layer // 2:layer // 2 + 1, :], gh)
    o_ref[0] += gate * _dot(z, wout_ref[...])

    if last:
        @pl.when(hd == RET_HEADS - 1)
        def _():
            o_ref[0] = _rms(o_ref[0], fg_ref[...])


def _ret_sample_layer(x, mods, ng, w_in, ld, gn, w_out, state_ret, cos, sin, final_g, *, i):
    B, L, _ = x.shape
    j = i // 2
    last = final_g is not None
    x_spec = pl.BlockSpec((1, L, D_MODEL), lambda b, hd: (b, 0, 0))
    kb = D_MODEL // RET_DK
    vb = 2 * D_MODEL // RET_DV
    in_specs = [
        x_spec,
        pl.BlockSpec((None, 1, 3, D_MODEL), lambda b, hd: (i, 1 + b, 0, 0)),
        _const_spec(ng.shape),
        pl.BlockSpec((D_MODEL, RET_DK), lambda b, hd: (0, hd)),
        pl.BlockSpec((D_MODEL, RET_DK), lambda b, hd: (0, kb + hd)),
        pl.BlockSpec((D_MODEL, RET_DV), lambda b, hd: (0, vb + hd)),
        pl.BlockSpec((D_MODEL, RET_DV), lambda b, hd: (0, vb + RET_HEADS + hd)),
        pl.BlockSpec(memory_space=pltpu.SMEM),
        pl.BlockSpec((gn.shape[0], RET_DV), lambda b, hd: (0, hd)),
        pl.BlockSpec((RET_DV, D_MODEL), lambda b, hd: (hd, 0)),
        pl.BlockSpec((1, 1, 2, 1, RET_DK, RET_DV), lambda b, hd: (b, j, 0, hd, 0, 0)),
        _const_spec((L, RET_DK // 2)),
        _const_spec((L, RET_DK // 2)),
    ]
    args = [x, mods, ng, w_in, w_in, w_in, w_in, ld, gn, w_out, state_ret, cos, sin]
    if last:
        in_specs.append(_const_spec((1, D_MODEL)))
        args.append(final_g)
    return pl.pallas_call(
        functools.partial(_ret_sample_kernel, last=last, layer=i),
        grid=(B, RET_HEADS),
        in_specs=in_specs,
        out_specs=x_spec,
        out_shape=jax.ShapeDtypeStruct(x.shape, F32),
        scratch_shapes=[pltpu.VMEM((L, D_MODEL), BF16)],
        compiler_params=pltpu.CompilerParams(
            dimension_semantics=("arbitrary", "arbitrary"), vmem_limit_bytes=VMEM_LIMIT),
        name="ret_sample",
    )(*args)


def _rope_tables(L):
    rows = L // GRID_W
    r = np.repeat(np.arange(rows, dtype=np.float32), GRID_W)
    col = np.tile(np.arange(GRID_W, dtype=np.float32), rows)
    nf = RET_DK // 4
    inv = (ROPE_BASE ** (-np.arange(nf, dtype=np.float32) / nf)).astype(np.float32)
    ang = np.concatenate([r[:, None] * inv, col[:, None] * inv], axis=-1).astype(np.float32)
    return jnp.asarray(np.cos(ang), F32), jnp.asarray(np.sin(ang), F32)


def kernel(x_prompt, x_sample, c, state_ret, c_ctx, ada_w, ada_b, norm_g, final_norm_g, conv_w_in, conv_dw, conv_dw_b, conv_ln_g, conv_ln_b, conv_w_out, ret_w_in, ret_log_decay, ret_gn_g, ret_w_out):
    n_dec = c.shape[0]
    cond8 = jnp.concatenate([c_ctx[None, :], c, jnp.zeros((SUBLANES - 1 - n_dec, D_MODEL), F32)], axis=0)
    mods = _adaln(cond8, ada_w, ada_b).reshape(DEPTH, SUBLANES, 3, D_MODEL)
    cos, sin = _rope_tables(x_sample.shape[1])
    fg = final_norm_g.reshape(1, D_MODEL)

    ng = norm_g
    conv_rest = (conv_dw, conv_dw_b, conv_ln_g, conv_ln_b)
    gn = ret_gn_g

    xp, xs = x_prompt, x_sample
    states = None
    cw_in, cw_out = _cast_conv_weights(conv_w_in, conv_w_out, 0)
    for i in range(DEPTH):
        if i % 2 == 0:
            xp, rw_in, rw_out = _conv_layer(xp, mods, ng, cw_in, *conv_rest, cw_out, (ret_w_in, ret_w_out),
                                            i=i, mod_row=0)
            xs, = _conv_layer(xs, mods, ng, cw_in, *conv_rest, cw_out, i=i, mod_row=1)
        else:
            final_g = fg if i == DEPTH - 1 else None
            res = _ret_prompt_layer(xp, mods, ng, rw_in, ret_log_decay, gn, rw_out, final_g, states,
                                    (conv_w_in, conv_w_out), i=i)
            xp, states = res[0], res[1]
            if final_g is None:
                cw_in, cw_out = res[2], res[3]
            xs = _ret_sample_layer(xs, mods, ng, rw_in, ret_log_decay, gn, rw_out, state_ret, cos, sin,
                                   final_g, i=i)
    return (xp, xs, states)
```

```python
import functools

import jax
import jax.numpy as jnp
import numpy as np
from jax import lax
from jax.experimental import pallas as pl
from jax.experimental.pallas import tpu as pltpu

D_MODEL = 1024
D_INNER = 2048
DEPTH = 4
CONV_K = 31
HALO = 16
RET_HEADS = 4
RET_DK = 256
RET_DV = 512
GRID_W = 64
ROPE_BASE = 10000.0
EPS = 1e-6
TILE = 256
LANES = 128
SUBLANES = 8
CHUNK = 256
SEQS_PER_STEP = 2
VMEM_LIMIT = 56 * 1024 * 1024

F32 = jnp.float32
BF16 = jnp.bfloat16


def _dot(a, b):
    return jnp.dot(a, b, preferred_element_type=F32)


def _dot_nt(a, b):
    return lax.dot_general(a, b, (((1,), (1,)), ((), ())), preferred_element_type=F32)


def _sigmoid(x):
    return 1.0 / (1.0 + jnp.exp(-x))


def _silu(x):
    return x * _sigmoid(x)


def _rms(x, g):
    return x * lax.rsqrt(jnp.mean(x * x, axis=-1, keepdims=True) + EPS) * g


def _mod_norm(x, g, mod_ref):
    r = lax.rsqrt(jnp.mean(x * x, axis=-1, keepdims=True) + EPS)
    return x * r * (g * (1.0 + mod_ref[0, 1:2, :])) + mod_ref[0, 0:1, :]


def _adaln_kernel(cond_ref, w_ref, b_ref, o_ref):
    s = _silu(cond_ref[...]).astype(BF16)
    o_ref[0] = _dot(s, w_ref[0].astype(BF16)) + b_ref[pl.ds(pl.program_id(0), 1), :]


def _adaln(cond8, ada_w, ada_b):
    return pl.pallas_call(
        _adaln_kernel,
        grid=(DEPTH,),
        in_specs=[
            pl.BlockSpec((SUBLANES, D_MODEL), lambda i: (0, 0)),
            pl.BlockSpec((1, D_MODEL, 3 * D_MODEL), lambda i: (i, 0, 0)),
            pl.BlockSpec((DEPTH, 3 * D_MODEL), lambda i: (0, 0)),
        ],
        out_specs=pl.BlockSpec((1, SUBLANES, 3 * D_MODEL), lambda i: (i, 0, 0)),
        out_shape=jax.ShapeDtypeStruct((DEPTH, SUBLANES, 3 * D_MODEL), F32),
        compiler_params=pltpu.CompilerParams(
            dimension_semantics=("arbitrary",), vmem_limit_bytes=VMEM_LIMIT),
        name="adaln",
    )(cond8, ada_w, ada_b)


def _cast_job(ws, layer, steps, step_of):
    in_specs, out_specs, out_shape = [], [], []
    for w in ws:
        rows, cols = w.shape[1] // steps, w.shape[2]
        in_specs.append(pl.BlockSpec((None, rows, cols), lambda *g: (layer, step_of(*g), 0)))
        out_specs.append(pl.BlockSpec((rows, cols), lambda *g: (step_of(*g), 0)))
        out_shape.append(jax.ShapeDtypeStruct(w.shape[1:], BF16))
    return in_specs, out_specs, out_shape


def _cast_kernel(a_ref, b_ref, ao_ref, bo_ref):
    ao_ref[...] = a_ref[...].astype(BF16)
    bo_ref[...] = b_ref[...].astype(BF16)


def _cast_conv_weights(w_in, w_out, j, steps=8):
    specs_in, specs_out, shapes = _cast_job((w_in, w_out), j, steps, lambda s: s)
    return pl.pallas_call(
        _cast_kernel, grid=(steps,), in_specs=specs_in, out_specs=specs_out, out_shape=shapes,
        compiler_params=pltpu.CompilerParams(dimension_semantics=("arbitrary",), vmem_limit_bytes=VMEM_LIMIT),
        name="cast_conv_weights",
    )(w_in, w_out)


def _exact_zero_after(v):
    bits = pltpu.bitcast(v, jnp.uint32)
    return pltpu.bitcast((bits >> 16) >> 16, F32)


def _conv_kernel(*refs, halo, tiles_per_seq, layer):
    if halo:
        (x_ref, xt_ref, xb_ref, mod_ref, ng_ref, win_ref, dw_ref, dwb_ref, lng_ref, lnb_ref,
         wout_ref, o_ref, pad_ref, u_ref, gate_ref) = refs
    else:
        (x_ref, mod_ref, ng_ref, win_ref, dw_ref, dwb_ref, lng_ref, lnb_ref, wout_ref, rwin_ref, rwout_ref,
         o_ref, rwin_o_ref, rwout_o_ref, pad_ref, u_ref, gate_ref) = refs
        rwin_o_ref[...] = rwin_ref[...].astype(BF16)
        rwout_o_ref[...] = rwout_ref[...].astype(BF16)
    n_seq, seg = x_ref.shape[0], x_ref.shape[1]
    n_slab = CHUNK // LANES
    n_chunk = D_INNER // CHUNK
    j = layer // 2
    ng = ng_ref[layer:layer + 1, :]
    h = _mod_norm(x_ref[...].reshape(n_seq * seg, D_MODEL), ng, mod_ref).astype(BF16)
    if halo:
        t = pl.program_id(1)
        ht = _mod_norm(xt_ref[0], ng, mod_ref).astype(BF16)
        hb = _mod_norm(xb_ref[0], ng, mod_ref).astype(BF16)
        hext = jnp.concatenate([ht, h, hb], axis=0)
        row = lax.broadcasted_iota(jnp.int32, (seg + 2 * HALO, CHUNK), 0)
        valid = jnp.logical_and(jnp.logical_or(row >= HALO, t > 0),
                                jnp.logical_or(row < HALO + seg, t < tiles_per_seq - 1))
    else:
        hext = h
        zeros = jnp.zeros((HALO, LANES), F32)
        for s in range(D_INNER // LANES):
            for q in range(n_seq):
                pad_ref[s, q, 0:HALO, :] = zeros
                pad_ref[s, q, HALO + seg:2 * HALO + seg, :] = zeros

    def finish(f):
        rows = slice(f * TILE, (f + 1) * TILE)
        q, r0 = divmod(f * TILE, seg)
        u = u_ref[rows, :]
        mu = jnp.mean(u, axis=-1, keepdims=True)
        d = u - mu
        var = jnp.mean(d * d, axis=-1, keepdims=True)
        y = d * lax.rsqrt(var + EPS) * lng_ref[j:j + 1, :] + lnb_ref[j:j + 1, :]
        z = _silu(y.astype(BF16)) * gate_ref[rows, :]
        out = x_ref[q, r0:r0 + TILE, :] + mod_ref[0, 2:3, :] * _dot(z, wout_ref[...])
        o_ref[q, r0:r0 + TILE, :] = out
        return out

    n_fin = n_seq * seg // TILE
    pin = None
    for c in range(n_chunk):
        c0 = c * CHUNK
        a = _dot(hext, win_ref[:, c0:c0 + CHUNK])
        b = _dot(hext, win_ref[:, D_INNER + c0:D_INNER + c0 + CHUNK])
        g = _dot(h, win_ref[:, 2 * D_INNER + c0:2 * D_INNER + c0 + CHUNK])
        u = a * _sigmoid(b)
        gate_ref[:, c0:c0 + CHUNK] = _silu(g.astype(BF16))
        if halo:
            u = jnp.where(valid, u, 0.0)
        for l in range(n_slab):
            s = c * n_slab + l
            if halo:
                pad_ref[s, 0, :, :] = u[:, l * LANES:(l + 1) * LANES]
            else:
                for q in range(n_seq):
                    pad_ref[s, q, HALO:HALO + seg, :] = u[q * seg:(q + 1) * seg, l * LANES:(l + 1) * LANES]
        for f in range(n_fin):
            if f > 0 and c == n_chunk - 1:
                pin = _exact_zero_after(finish(f - 1)[TILE - SUBLANES:TILE, D_MODEL - LANES:D_MODEL])[0:1, :]
            for l in range(n_slab):
                s = c * n_slab + l
                lanes = slice(c0 + l * LANES, c0 + (l + 1) * LANES)
                bias = dwb_ref[j:j + 1, lanes]
                if f > 0 and c == n_chunk - 1:
                    bias = bias + pin
                for rb in range(TILE // LANES):
                    q, r0 = divmod(f * TILE + rb * LANES, seg)
                    acc = jnp.broadcast_to(bias, (LANES, LANES))
                    for k in range(CONV_K):
                        start = r0 + k + HALO - CONV_K // 2
                        acc = acc + pad_ref[s, q, pl.ds(start, LANES), :] * dw_ref[k:k + 1, lanes]
                    u_ref[f * TILE + rb * LANES:f * TILE + (rb + 1) * LANES, lanes] = acc
    finish(n_fin - 1)


def _const_spec(shape, layer=None):
    nd = len(shape)
    if layer is None:
        return pl.BlockSpec(shape, lambda *_: (0,) * nd, pipeline_mode=pl.Buffered(1))
    return pl.BlockSpec((None,) + shape, lambda *_: (layer,) + (0,) * nd, pipeline_mode=pl.Buffered(1))


def _conv_layer(x, mods, ng, w_in, dw, dwb, lng, lnb, w_out, next_ret_w=None, *, i, mod_row):
    B, L, _ = x.shape
    j = i // 2
    halo = L > TILE
    n_seq = 1 if halo else SEQS_PER_STEP
    seg = SEQS_PER_STEP * TILE // n_seq
    tiles = L // seg
    hb = seg // HALO
    x_spec = pl.BlockSpec((n_seq, seg, D_MODEL), lambda b, t: (b, t, 0))
    in_specs = [x_spec]
    args = [x]
    if halo:
        in_specs += [
            pl.BlockSpec((1, HALO, D_MODEL), lambda b, t: (b, jnp.maximum(t * hb - 1, 0), 0)),
            pl.BlockSpec((1, HALO, D_MODEL), lambda b, t: (b, jnp.minimum((t + 1) * hb, L // HALO - 1), 0)),
        ]
        args += [x, x]
    mod_map = (lambda b, t: (i, mod_row + b, 0, 0)) if mod_row else (lambda b, t: (i, 0, 0, 0))
    in_specs += [
        pl.BlockSpec((None, 1, 3, D_MODEL), mod_map),
        _const_spec(ng.shape),
        _const_spec((D_MODEL, 3 * D_INNER)),
        _const_spec((CONV_K, D_INNER), j),
        _const_spec(dwb.shape),
        _const_spec(lng.shape),
        _const_spec(lnb.shape),
        _const_spec((D_INNER, D_MODEL)),
    ]
    args += [mods, ng, w_in, dw, dwb, lng, lnb, w_out]
    out_specs, out_shape = [x_spec], [jax.ShapeDtypeStruct(x.shape, F32)]
    if not halo:
        job = _cast_job(next_ret_w, j, B // n_seq, lambda b, t: b)
        in_specs += job[0]
        args += list(next_ret_w)
        out_specs += job[1]
        out_shape += job[2]
    return pl.pallas_call(
        functools.partial(_conv_kernel, halo=halo, tiles_per_seq=tiles, layer=i),
        grid=(B // n_seq, tiles),
        in_specs=in_specs,
        out_specs=out_specs,
        out_shape=out_shape,
        scratch_shapes=[
            pltpu.VMEM((D_INNER // LANES, n_seq, seg + 2 * HALO, LANES), F32),
            pltpu.VMEM((n_seq * seg, D_INNER), F32),
            pltpu.VMEM((n_seq * seg, D_INNER), BF16),
        ],
        compiler_params=pltpu.CompilerParams(
            dimension_semantics=("arbitrary", "arbitrary"), vmem_limit_bytes=VMEM_LIMIT),
        name="conv_sample" if halo else "conv_prompt",
    )(*args)


def _decay_mask(diff, lgf, lgb):
    e = jnp.exp(jnp.abs(diff) * jnp.where(diff >= 0.0, lgf, lgb))
    return e * jnp.where(diff == 0.0, 2.0 * RET_DK ** -0.5, RET_DK ** -0.5)


def _head_norm_gate(o, gn, g):
    mu = jnp.mean(o, axis=-1, keepdims=True)
    d = o - mu
    var = jnp.mean(d * d, axis=-1, keepdims=True)
    return (d * lax.rsqrt(var + EPS) * gn * _silu(g)).astype(BF16)


def _ret_prompt_kernel(*refs, last, layer):
    if last:
        (x_ref, mod_ref, ng_ref, win_ref, ld_ref, gn_ref, wout_ref, fg_ref, _st_in,
         o_ref, st_ref, m_ref, w_ref, z_ref) = refs
    else:
        (x_ref, mod_ref, ng_ref, win_ref, ld_ref, gn_ref, wout_ref, cwin_ref, cwout_ref,
         o_ref, st_ref, cwin_o_ref, cwout_o_ref, m_ref, w_ref, z_ref) = refs
        st_ref[0, 1:] = jnp.zeros((st_ref.shape[1] - 1,) + st_ref.shape[2:], F32)
        cwin_o_ref[...] = cwin_ref[...].astype(BF16)
        cwout_o_ref[...] = cwout_ref[...].astype(BF16)
    L = TILE

    @pl.when(pl.program_id(0) == 0)
    def _():
        lg = -jnp.exp(ld_ref[...])
        ii = lax.broadcasted_iota(jnp.int32, (L, L), 0)
        jj = lax.broadcasted_iota(jnp.int32, (L, L), 1)
        diff = (ii - jj).astype(F32)
        pos = lax.broadcasted_iota(jnp.int32, (1, L), 1).astype(F32)
        for hd in range(RET_HEADS):
            lgf = lg[0:1, hd:hd + 1]
            lgb = lg[1:2, hd:hd + 1]
            m_ref[hd] = _decay_mask(diff, lgf, lgb)
            w_ref[2 * hd:2 * hd + 1, :] = jnp.exp((L - 1.0 - pos) * lgf) * RET_DK ** -0.5
            w_ref[2 * hd + 1:2 * hd + 2, :] = jnp.exp(pos * lgb) * RET_DK ** -0.5

    x = x_ref[0]
    j = layer // 2
    h = _mod_norm(x, ng_ref[layer:layer + 1, :], mod_ref).astype(BF16)
    q = _dot(h, win_ref[:, 0:D_MODEL]).astype(BF16)
    k = _dot(h, win_ref[:, D_MODEL:2 * D_MODEL])
    v = _dot(h, win_ref[:, 2 * D_MODEL:2 * D_MODEL + D_INNER]).astype(BF16)
    g = _dot(h, win_ref[:, 2 * D_MODEL + D_INNER:])
    kb = k.astype(BF16)
    for hd in range(RET_HEADS):
        kcols = slice(hd * RET_DK, (hd + 1) * RET_DK)
        vcols = slice(hd * RET_DV, (hd + 1) * RET_DV)
        vh = v[:, vcols]
        s = _dot_nt(q[:, kcols], kb[:, kcols])
        p = (s * m_ref[hd]).astype(BF16)
        o = _dot(p, vh)
        z_ref[:, vcols] = _head_norm_gate(o, gn_ref[j:j + 1, vcols], g[:, vcols])
        kt = k[:, kcols].T
        ktw = jnp.concatenate([(kt * w_ref[2 * hd:2 * hd + 1, :]).astype(BF16),
                               (kt * w_ref[2 * hd + 1:2 * hd + 2, :]).astype(BF16)], axis=0)
        st = _dot(ktw, vh)
        slot = 0 if last else layer // 2
        st_ref[0, slot, 0, hd] = st[:RET_DK]
        st_ref[0, slot, 1, hd] = st[RET_DK:]
    xn = x + mod_ref[0, 2:3, :] * _dot(z_ref[...], wout_ref[...])
    if last:
        xn = _rms(xn, fg_ref[...])
    o_ref[0] = xn


def _ret_prompt_layer(x, mods, ng, w_in, ld, gn, w_out, final_g, states, next_conv_w, *, i):
    B, L, _ = x.shape
    j = i // 2
    last = final_g is not None
    x_spec = pl.BlockSpec((1, L, D_MODEL), lambda b: (b, 0, 0))
    st_shape = (B, DEPTH // 2, 2, RET_HEADS, RET_DK, RET_DV)
    in_specs = [
        x_spec,
        pl.BlockSpec((None, 1, 3, D_MODEL), lambda b: (i, 0, 0, 0)),
        _const_spec(ng.shape),
        _const_spec((D_MODEL, 2 * D_MODEL + 2 * D_INNER)),
        _const_spec((2, RET_HEADS), j),
        _const_spec(gn.shape),
        _const_spec((D_INNER, D_MODEL)),
    ]
    args = [x, mods, ng, w_in, ld, gn, w_out]
    aliases = {}
    if last:
        in_specs += [_const_spec((1, D_MODEL)), pl.BlockSpec(memory_space=pl.ANY)]
        args += [final_g, states]
        aliases = {len(args) - 1: 1}
    st_block = (1, 1) if last else (1, DEPTH // 2)
    out_specs = [x_spec, pl.BlockSpec(st_block + (2, RET_HEADS, RET_DK, RET_DV),
                                      lambda b: (b, j if last else 0, 0, 0, 0, 0))]
    out_shape = [jax.ShapeDtypeStruct(x.shape, F32), jax.ShapeDtypeStruct(st_shape, F32)]
    if not last:
        job = _cast_job(next_conv_w, j + 1, B, lambda b: b)
        in_specs += job[0]
        args += list(next_conv_w)
        out_specs += job[1]
        out_shape += job[2]
    return pl.pallas_call(
        functools.partial(_ret_prompt_kernel, last=last, layer=i),
        grid=(B,),
        in_specs=in_specs,
        out_specs=out_specs,
        out_shape=out_shape,
        scratch_shapes=[
            pltpu.VMEM((RET_HEADS, L, L), F32),
            pltpu.VMEM((2 * RET_HEADS, L), F32),
            pltpu.VMEM((L, D_INNER), BF16),
        ],
        input_output_aliases=aliases,
        compiler_params=pltpu.CompilerParams(
            dimension_semantics=("arbitrary",), vmem_limit_bytes=VMEM_LIMIT),
        name="ret_prompt",
    )(*args)


def _ret_sample_kernel(*refs, last, layer):
    if last:
        (x_ref, mod_ref, ng_ref, wq_ref, wk_ref, wv_ref, wg_ref, ld_ref, gn_ref, wout_ref, s0_ref,
         cos_ref, sin_ref, fg_ref, o_ref, h_ref) = refs
    else:
        (x_ref, mod_ref, ng_ref, wq_ref, wk_ref, wv_ref, wg_ref, ld_ref, gn_ref, wout_ref, s0_ref,
         cos_ref, sin_ref, o_ref, h_ref) = refs
    L = x_ref.shape[1]
    half = RET_DK // 2
    hd = pl.program_id(1)

    @pl.when(hd == 0)
    def _():
        x = x_ref[0]
        h_ref[...] = _mod_norm(x, ng_ref[layer:layer + 1, :], mod_ref).astype(BF16)
        o_ref[0] = x

    h = h_ref[...]
    cos = cos_ref[...]
    sin = sin_ref[...]

    def rope(t):
        t1, t2 = t[:, :half], t[:, half:]
        return jnp.concatenate([t1 * cos - t2 * sin, t2 * cos + t1 * sin], axis=-1)

    qh = rope(_dot(h, wq_ref[...]))
    kh = rope(_dot(h, wk_ref[...])).astype(BF16)
    vh = _dot(h, wv_ref[...]).astype(BF16)
    gh = _dot(h, wg_ref[...])
    posq = lax.broadcasted_iota(jnp.int32, (L, RET_DK), 0).astype(F32)
    lgf_q = -jnp.exp(jnp.full((1, RET_DK), ld_ref[layer // 2, 0, hd], F32))
    lgb_q = -jnp.exp(jnp.full((1, RET_DK), ld_ref[layer // 2, 1, hd], F32))
    ox = (_dot((qh * jnp.exp((posq + 1.0) * lgf_q)).astype(BF16), s0_ref[0, 0, 0, 0].astype(BF16))
          + _dot((qh * jnp.exp((L - posq) * lgb_q)).astype(BF16), s0_ref[0, 0, 1, 0].astype(BF16)))
    qb = qh.astype(BF16)
    gate = mod_ref[0, 2:3, :]
    n_blk = L // TILE
    ii = lax.broadcasted_iota(jnp.int32, (TILE, TILE), 0)
    jj = lax.broadcasted_iota(jnp.int32, (TILE, TILE), 1)
    diff = (ii - jj).astype(F32)
    m_diag = _decay_mask(diff, lgf_q, lgb_q)
    below = [jnp.exp((TILE + diff) * lgf_q) * RET_DK ** -0.5]
    above = [jnp.exp((TILE - diff) * lgb_q) * RET_DK ** -0.5]
    for dist in range(1, n_blk - 1):
        below.append(below[0] * jnp.exp((TILE * dist) * lgf_q))
        above.append(above[0] * jnp.exp((TILE * dist) * lgb_q))
    m = jnp.concatenate(
        [jnp.concatenate([m_diag if cb == r else below[r - cb - 1] if cb < r else above[cb - r - 1]
                          for cb in range(n_blk)], axis=1) for r in range(n_blk)], axis=0)
    p = (_dot_nt(qb, kh) * m).astype(BF16)
    o = _dot(p, vh) + ox
    z = _head_norm_gate(o, gn_ref[layer // 2:layer // 2 + 1, :], gh)
    o_ref[0] += gate * _dot(z, wout_ref[...])

    if last:
        @pl.when(hd == RET_HEADS - 1)
        def _():
            o_ref[0] = _rms(o_ref[0], fg_ref[...])


def _ret_sample_layer(x, mods, ng, w_in, ld, gn, w_out, state_ret, cos, sin, final_g, *, i):
    B, L, _ = x.shape
    j = i // 2
    last = final_g is not None
    x_spec = pl.BlockSpec((1, L, D_MODEL), lambda b, hd: (b, 0, 0))
    kb = D_MODEL // RET_DK
    vb = 2 * D_MODEL // RET_DV
    in_specs = [
        x_spec,
        pl.BlockSpec((None, 1, 3, D_MODEL), lambda b, hd: (i, 1 + b, 0, 0)),
        _const_spec(ng.shape),
        pl.BlockSpec((D_MODEL, RET_DK), lambda b, hd: (0, hd)),
        pl.BlockSpec((D_MODEL, RET_DK), lambda b, hd: (0, kb + hd)),
        pl.BlockSpec((D_MODEL, RET_DV), lambda b, hd: (0, vb + hd)),
        pl.BlockSpec((D_MODEL, RET_DV), lambda b, hd: (0, vb + RET_HEADS + hd)),
        pl.BlockSpec(memory_space=pltpu.SMEM),
        pl.BlockSpec((gn.shape[0], RET_DV), lambda b, hd: (0, hd)),
        pl.BlockSpec((RET_DV, D_MODEL), lambda b, hd: (hd, 0)),
        pl.BlockSpec((1, 1, 2, 1, RET_DK, RET_DV), lambda b, hd: (b, j, 0, hd, 0, 0)),
        _const_spec((L, RET_DK // 2)),
        _const_spec((L, RET_DK // 2)),
    ]
    args = [x, mods, ng, w_in, w_in, w_in, w_in, ld, gn, w_out, state_ret, cos, sin]
    if last:
        in_specs.append(_const_spec((1, D_MODEL)))
        args.append(final_g)
    return pl.pallas_call(
        functools.partial(_ret_sample_kernel, last=last, layer=i),
        grid=(B, RET_HEADS),
        in_specs=in_specs,
        out_specs=x_spec,
        out_shape=jax.ShapeDtypeStruct(x.shape, F32),
        scratch_shapes=[pltpu.VMEM((L, D_MODEL), BF16)],
        compiler_params=pltpu.CompilerParams(
            dimension_semantics=("arbitrary", "arbitrary"), vmem_limit_bytes=VMEM_LIMIT),
        name="ret_sample",
    )(*args)


def _rope_tables(L):
    rows = L // GRID_W
    r = np.repeat(np.arange(rows, dtype=np.float32), GRID_W)
    col = np.tile(np.arange(GRID_W, dtype=np.float32), rows)
    nf = RET_DK // 4
    inv = (ROPE_BASE ** (-np.arange(nf, dtype=np.float32) / nf)).astype(np.float32)
    ang = np.concatenate([r[:, None] * inv, col[:, None] * inv], axis=-1).astype(np.float32)
    return jnp.asarray(np.cos(ang), F32), jnp.asarray(np.sin(ang), F32)


def kernel(x_prompt, x_sample, c, state_ret, c_ctx, ada_w, ada_b, norm_g, final_norm_g, conv_w_in, conv_dw, conv_dw_b, conv_ln_g, conv_ln_b, conv_w_out, ret_w_in, ret_log_decay, ret_gn_g, ret_w_out):
    n_dec = c.shape[0]
    cond8 = jnp.concatenate([c_ctx[None, :], c, jnp.zeros((SUBLANES - 1 - n_dec, D_MODEL), F32)], axis=0)
    mods = _adaln(cond8, ada_w, ada_b).reshape(DEPTH, SUBLANES, 3, D_MODEL)
    cos, sin = _rope_tables(x_sample.shape[1])
    fg = final_norm_g.reshape(1, D_MODEL)

    ng = norm_g
    conv_rest = (conv_dw, conv_dw_b, conv_ln_g, conv_ln_b)
    gn = ret_gn_g

    xp, xs = x_prompt, x_sample
    states = None
    cw_in, cw_out = _cast_conv_weights(conv_w_in, conv_w_out, 0)
    for i in range(DEPTH):
        if i % 2 == 0:
            xp, rw_in, rw_out = _conv_layer(xp, mods, ng, cw_in, *conv_rest, cw_out, (ret_w_in, ret_w_out),
                                            i=i, mod_row=0)
            xs, = _conv_layer(xs, mods, ng, cw_in, *conv_rest, cw_out, i=i, mod_row=1)
        else:
            final_g = fg if i == DEPTH - 1 else None
            res = _ret_prompt_layer(xp, mods, ng, rw_in, ret_log_decay, gn, rw_out, final_g, states,
                                    (conv_w_in, conv_w_out), i=i)
            xp, states = res[0], res[1]
            if final_g is None:
                cw_in, cw_out = res[2], res[3]
            xs = _ret_sample_layer(xs, mods, ng, rw_in, ret_log_decay, gn, rw_out, state_ret, cos, sin,
                                   final_g, i=i)
    return (xp, xs, states)
```
